```python
import math
import jax, jax.numpy as jnp
from jax import lax
import numpy as np

D_MODEL = 1024
BATCH = 8
SEQ = 8192
DEPTH = 2

HEAD_DIM = 64
M_HEADS = 4
M_DIM = 64
CONV_WIDTH = 4
CHUNK = 128
S_HEADS = 8
S_KV_HEADS = 2
WINDOW = 128
D_HEADS = 4
D_QK = 32
D_V = 64
Q_BLOCK = 128
ROPE_THETA = 10000.0
D_FF = 4 * D_MODEL
EPS = 1e-6

M_WIDTH = M_HEADS * M_DIM
S_WIDTH = S_HEADS * HEAD_DIM
S_KV_WIDTH = S_KV_HEADS * HEAD_DIM
D_QK_WIDTH = D_HEADS * 2 * D_QK
D_WIDTH = D_HEADS * D_V
MIX_WIDTH = M_WIDTH + S_WIDTH + D_WIDTH
SPLIT_SIZES = (M_WIDTH, M_WIDTH, M_WIDTH, M_WIDTH, M_HEADS, M_HEADS,
               S_WIDTH, S_KV_WIDTH, S_KV_WIDTH,
               D_QK_WIDTH, D_QK_WIDTH, D_WIDTH)
IN_WIDTH = sum(SPLIT_SIZES)

kernel_name = 'hybrid_mlstm_swa_diffattn_block'


def rms_norm(x, g):
    xf = x.astype(jnp.float32)
    y = xf * lax.rsqrt(jnp.mean(xf * xf, axis=-1, keepdims=True) + EPS)
    return (y * g.astype(jnp.float32)).astype(x.dtype)


def rope(x, pos):
    d = x.shape[-1]
    half = d // 2
    inv = ROPE_THETA ** (-jnp.arange(half, dtype=jnp.float32) * 2.0 / d)
    ang = pos.astype(jnp.float32)[:, None] * inv[None, :]
    cos = jnp.cos(ang)[None, :, None, :]
    sin = jnp.sin(ang)[None, :, None, :]
    xf = x.astype(jnp.float32)
    x1, x2 = xf[..., :half], xf[..., half:]
    return jnp.concatenate([x1 * cos - x2 * sin, x2 * cos + x1 * sin], axis=-1).astype(x.dtype)


def split_columns(z):
    parts = []
    start = 0
    for size in SPLIT_SIZES:
        parts.append(z[..., start:start + size])
        start += size
    return parts


def causal_conv(x, w, b):
    K = w.shape[0]
    S = x.shape[1]
    xp = jnp.pad(x, ((0, 0), (K - 1, 0), (0, 0)))
    y = xp[:, 0:S, :] * w[0]
    for j in range(1, K):
        y = y + xp[:, j:j + S, :] * w[j]
    return y + b


def mlstm_chunkwise(q, k, v, i_pre, f_pre):
    B, S, H, D = q.shape
    L = CHUNK
    nC = S // L
    qf = q.astype(jnp.float32)
    kf = k.astype(jnp.float32) * (D ** -0.5)
    vf = v.astype(jnp.float32)
    logf = jax.nn.log_sigmoid(f_pre.astype(jnp.float32))
    ig = i_pre.astype(jnp.float32)

    def to_chunks(a):
        a = a.reshape((B, nC, L, H) + a.shape[3:])
        return jnp.moveaxis(a, (1, 3), (0, 2))

    causal = jnp.tril(jnp.ones((L, L), dtype=bool))

    def step(carry, xs):
        C, n, m = carry
        qc, kc, vc, ic, fc = xs
        b = jnp.cumsum(fc, axis=-1)
        dmat = b[..., :, None] - b[..., None, :] + ic[..., None, :]
        dmat = jnp.where(causal, dmat, -jnp.inf)
        inter = b + m[..., None]
        m_t = jnp.maximum(inter, jnp.max(dmat, axis=-1))
        w_intra = jnp.exp(dmat - m_t[..., None])
        sc = jnp.einsum('bhtd,bhsd->bhts', qc, kc) * w_intra
        a_inter = jnp.exp(inter - m_t)
        num = (a_inter[..., None] * jnp.einsum('bhvk,bhtk->bhtv', C, qc)
               + jnp.einsum('bhts,bhsv->bhtv', sc, vc))
        den = a_inter * jnp.einsum('bhk,bhtk->bht', n, qc) + jnp.sum(sc, axis=-1)
        h = num / jnp.maximum(jnp.abs(den), jnp.exp(-m_t))[..., None]
        b_last = b[..., -1]
        g = b_last[..., None] - b + ic
        m_new = jnp.maximum(b_last + m, jnp.max(g, axis=-1))
        w_state = jnp.exp(g - m_new[..., None])
        decay = jnp.exp(b_last + m - m_new)
        C_new = decay[..., None, None] * C + jnp.einsum('bhs,bhsv,bhsk->bhvk', w_state, vc, kc)
        n_new = decay[..., None] * n + jnp.einsum('bhs,bhsk->bhk', w_state, kc)
        return (C_new, n_new, m_new), h

    init = (jnp.zeros((B, H, D, D), jnp.float32),
            jnp.zeros((B, H, D), jnp.float32),
            jnp.zeros((B, H), jnp.float32))
    xs = (to_chunks(qf), to_chunks(kf), to_chunks(vf), to_chunks(ig), to_chunks(logf))
    _, h = lax.scan(step, init, xs)
    return jnp.moveaxis(h, (0, 2), (1, 3)).reshape(B, S, H, D)


def sliding_window_gqa(q, k, v, sinks):
    B, S, Hq, D = q.shape
    Hkv = k.shape[2]
    G = Hq // Hkv
    W = WINDOW
    nB = S // W
    qb = q.reshape(B, nB, W, Hkv, G, D)

    def band(a):
        ab = a.reshape(B, nB, W, Hkv, D)
        prev = jnp.pad(ab, ((0, 0), (1, 0), (0, 0), (0, 0), (0, 0)))[:, :-1]
        return jnp.concatenate([prev, ab], axis=2)

    kb, vb = band(k), band(v)
    s = jnp.einsum('bnqhgd,bnkhd->bnhgqk', qb, kb).astype(jnp.float32) * (D ** -0.5)
    qpos = jnp.arange(W)[:, None] + W
    kpos = jnp.arange(2 * W)[None, :]
    rel = qpos - kpos
    valid = (rel >= 0) & (rel < W)
    valid = valid[None] & ((jnp.arange(nB)[:, None, None] > 0) | (kpos[None] >= W))
    s = jnp.where(valid[None, :, None, None], s, -jnp.inf)
    sink = sinks.astype(jnp.float32).reshape(Hkv, G)[None, None, :, :, None]
    mx = jnp.maximum(jnp.max(s, axis=-1), sink)
    e = jnp.exp(s - mx[..., None])
    denom = jnp.sum(e, axis=-1) + jnp.exp(sink - mx)
    p = e / denom[..., None]
    o = jnp.einsum('bnhgqk,bnkhd->bnqhgd', p.astype(v.dtype), vb)
    return o.reshape(B, S, Hq, D)


def differential_attention(q, k, v, lam):
    B, S, H, _, Dk = q.shape
    nQ = S // Q_BLOCK
    qb = jnp.moveaxis(q.reshape(B, nQ, Q_BLOCK, H, 2, Dk), 1, 0)
    kpos = jnp.arange(S)

    def block(args):
        qblk, i = args
        s = jnp.einsum('bqhmd,bkhmd->bhmqk', qblk, k).astype(jnp.float32) * (Dk ** -0.5)
        qpos = i * Q_BLOCK + jnp.arange(Q_BLOCK)
        s = jnp.where(kpos[None, :] <= qpos[:, None], s, -jnp.inf)
        p = jax.nn.softmax(s, axis=-1)
        a = p[:, :, 0] - lam * p[:, :, 1]
        return jnp.einsum('bhqk,bkhd->bqhd', a.astype(v.dtype), v)

    o = lax.map(block, (qb, jnp.arange(nQ)))
    return jnp.moveaxis(o, 0, 1).reshape(B, S, H, v.shape[-1])


def hybrid_layer(x, layer_idx, w_in, conv_w, conv_b, i_bias, f_bias, m_norm_g, sinks,
                 lam_q1, lam_k1, lam_q2, lam_k2, sub_g, w_out, w_up, w_down,
                 g_pre_mix, g_post_mix, g_pre_mlp, g_post_mlp):
    B, S, _ = x.shape
    pos = jnp.arange(S)
    h = rms_norm(x, g_pre_mix)
    z = h @ w_in
    mq, mk, mv, mo, mi, mf, sq, sk, sv, dq, dk, dv = split_columns(z)

    qk = jax.nn.silu(causal_conv(jnp.concatenate([mq, mk], axis=-1), conv_w, conv_b))
    m_q = qk[..., :M_WIDTH].reshape(B, S, M_HEADS, M_DIM)
    m_k = qk[..., M_WIDTH:].reshape(B, S, M_HEADS, M_DIM)
    m_v = mv.reshape(B, S, M_HEADS, M_DIM)
    h_m = mlstm_chunkwise(m_q, m_k, m_v, mi + i_bias, mf + f_bias)
    h_m = rms_norm(h_m, m_norm_g.reshape(M_HEADS, M_DIM)).reshape(B, S, M_WIDTH)
    out_m = (jax.nn.sigmoid(mo.astype(jnp.float32)) * h_m).astype(x.dtype)

    s_q = rope(sq.reshape(B, S, S_HEADS, HEAD_DIM), pos)
    s_k = rope(sk.reshape(B, S, S_KV_HEADS, HEAD_DIM), pos)
    s_v = sv.reshape(B, S, S_KV_HEADS, HEAD_DIM)
    out_s = sliding_window_gqa(s_q, s_k, s_v, sinks).reshape(B, S, S_WIDTH)

    d_q = rope(dq.reshape(B, S, D_HEADS * 2, D_QK), pos).reshape(B, S, D_HEADS, 2, D_QK)
    d_k = rope(dk.reshape(B, S, D_HEADS * 2, D_QK), pos).reshape(B, S, D_HEADS, 2, D_QK)
    d_v = dv.reshape(B, S, D_HEADS, D_V)
    lam_init = 0.8 - 0.6 * math.exp(-0.3 * layer_idx)
    lam = (jnp.exp(jnp.sum(lam_q1.astype(jnp.float32) * lam_k1.astype(jnp.float32)))
           - jnp.exp(jnp.sum(lam_q2.astype(jnp.float32) * lam_k2.astype(jnp.float32)))
           + lam_init)
    o_d = differential_attention(d_q, d_k, d_v, lam)
    out_d = (rms_norm(o_d, sub_g) * (1.0 - lam_init)).reshape(B, S, D_WIDTH).astype(x.dtype)

    mix = jnp.concatenate([out_m, out_s.astype(x.dtype), out_d], axis=-1) @ w_out
    x = x + rms_norm(mix, g_post_mix)

    h2 = rms_norm(x, g_pre_mlp)
    u = jnp.square(jax.nn.relu(h2 @ w_up))
    x = x + rms_norm(u @ w_down, g_post_mlp)
    return x


def setup_inputs(seed: int = 0) -> dict:
    key = jax.random.key(seed)
    ks = jax.random.split(key, 21)
    f32 = jnp.float32

    def nrm(k, shape, scale):
        return jax.random.normal(k, shape, f32) * scale

    def gain(k, shape):
        return 1.0 + 0.02 * jax.random.normal(k, shape, f32)

    return {
        'x': jax.random.normal(ks[0], (BATCH, SEQ, D_MODEL), f32),
        'w_in': nrm(ks[1], (DEPTH, D_MODEL, IN_WIDTH), D_MODEL ** -0.5),
        'conv_w': nrm(ks[2], (DEPTH, CONV_WIDTH, 2 * M_WIDTH), CONV_WIDTH ** -0.5),
        'conv_b': nrm(ks[3], (DEPTH, 2 * M_WIDTH), 0.01),
        'i_bias': nrm(ks[4], (DEPTH, M_HEADS), 0.1),
        'f_bias': 3.0 + nrm(ks[5], (DEPTH, M_HEADS), 0.5),
        'm_norm_g': gain(ks[6], (DEPTH, M_WIDTH)),
        'sinks': nrm(ks[7], (DEPTH, S_HEADS), 0.5),
        'lam_q1': nrm(ks[8], (DEPTH, D_QK), 0.1),
        'lam_k1': nrm(ks[9], (DEPTH, D_QK), 0.1),
        'lam_q2': nrm(ks[10], (DEPTH, D_QK), 0.1),
        'lam_k2': nrm(ks[11], (DEPTH, D_QK), 0.1),
        'sub_g': gain(ks[12], (DEPTH, D_V)),
        'w_out': nrm(ks[13], (DEPTH, MIX_WIDTH, D_MODEL), MIX_WIDTH ** -0.5),
        'w_up': nrm(ks[14], (DEPTH, D_MODEL, D_FF), D_MODEL ** -0.5),
        'w_down': nrm(ks[15], (DEPTH, D_FF, D_MODEL), D_FF ** -0.5),
        'g_pre_mix': gain(ks[16], (DEPTH, D_MODEL)),
        'g_post_mix': gain(ks[17], (DEPTH, D_MODEL)),
        'g_pre_mlp': gain(ks[18], (DEPTH, D_MODEL)),
        'g_post_mlp': gain(ks[19], (DEPTH, D_MODEL)),
    }


def reference(x, w_in, conv_w, conv_b, i_bias, f_bias, m_norm_g, sinks,
              lam_q1, lam_k1, lam_q2, lam_k2, sub_g, w_out, w_up, w_down,
              g_pre_mix, g_post_mix, g_pre_mlp, g_post_mlp):
    for l in range(DEPTH):
        x = hybrid_layer(x, l, w_in[l], conv_w[l], conv_b[l], i_bias[l], f_bias[l],
                         m_norm_g[l], sinks[l], lam_q1[l], lam_k1[l], lam_q2[l], lam_k2[l],
                         sub_g[l], w_out[l], w_up[l], w_down[l],
                         g_pre_mix[l], g_post_mix[l], g_pre_mlp[l], g_post_mlp[l])
    return x
```

```python
import functools
import math

import jax
import jax.numpy as jnp
import numpy as np
from jax import lax
from jax.experimental import pallas as pl
from jax.experimental.pallas import tpu as pltpu

D_MODEL = 1024
M_HEADS = 4
M_DIM = 64
CONV_WIDTH = 4
CHUNK = 128
S_HEADS = 8
S_KV_HEADS = 2
HEAD_DIM = 64
WINDOW = 128
D_HEADS = 4
D_QK = 32
D_V = 64
ROPE_THETA = 10000.0
D_FF = 4 * D_MODEL
EPS = 1e-6

M_WIDTH = M_HEADS * M_DIM
S_WIDTH = S_HEADS * HEAD_DIM
S_KV_WIDTH = S_KV_HEADS * HEAD_DIM
D_QK_WIDTH = D_HEADS * 2 * D_QK
D_WIDTH = D_HEADS * D_V

LANES = 128
VMEM_LIMIT = 56 * 1024 * 1024

LOG2E = 1.4426950408889634

BF16 = jnp.bfloat16
F32 = jnp.float32

OFF_QK = 0
OFF_MV = 512
OFF_MO = 768
OFF_G = 1024
OFF_SQ = 1152
OFF_SK = 1664
OFF_SV = 1792
OFF_DQ = 1920
OFF_DK = 2176
OFF_DV = 2432
PACKED_WIDTH = 2944


def _nt_dot(a, b):
    return lax.dot_general(a, b, (((1,), (1,)), ((), ())), preferred_element_type=F32)


def _dot(a, b):
    return jnp.dot(a, b, preferred_element_type=F32)


def _rms(x, g):
    return x * lax.rsqrt(jnp.mean(x * x, axis=-1, keepdims=True) + EPS) * g


def _sigmoid(x):
    return 1.0 / (1.0 + jnp.exp(-x))


def _log_sigmoid(x):
    return jnp.minimum(x, 0.0) - jnp.log(1.0 + jnp.exp(-jnp.abs(x)))


def _split3(x):
    hi = x.astype(BF16)
    r1 = x - hi.astype(F32)
    mid = r1.astype(BF16)
    lo = (r1 - mid.astype(F32)).astype(BF16)
    return hi, mid, lo


def _rope_block(zb, cos, sin_signed, first_half, half):
    fwd = pltpu.roll(zb, LANES - half, 1)
    bwd = pltpu.roll(zb, half, 1)
    return zb * cos + jnp.where(first_half, fwd, bwd) * sin_signed


def _in_proj_kernel(x_ref, g_ref, w_ref, wgt_ref, c64_ref, s64_ref, c32_ref, s32_ref,
                    qk_ref, mv_ref, mo_ref, gate_ref, gatet_ref,
                    sq_ref, sk_ref, sv_ref, dq_ref, dk_ref, dv_ref, h_scr):
    xf = x_ref[0]
    h_scr[...] = _rms(xf, g_ref[...]).astype(BF16)
    h = h_scr[...]

    def seg(off, width):
        return _dot(h, w_ref[:, off:off + width])

    qk_ref[0] = seg(OFF_QK, 512).astype(BF16)
    mv_ref[0] = seg(OFF_MV, 256).astype(BF16)
    mo_ref[0] = seg(OFF_MO, 256).astype(BF16)
    gate_ref[0] = seg(OFF_G, 128)
    gatet_ref[0] = _nt_dot(wgt_ref[...], h)

    lane = lax.broadcasted_iota(jnp.int32, (xf.shape[0], LANES), 1)
    first64 = (lane % 64) < 32
    first32 = (lane % 32) < 16
    c64, s64 = c64_ref[...], s64_ref[...]
    c32, s32 = c32_ref[...], s32_ref[...]

    sq = seg(OFF_SQ, 512)
    for c in range(4):
        blk = _rope_block(sq[:, c * LANES:(c + 1) * LANES], c64, s64, first64, 32)
        sq_ref[0, :, c * LANES:(c + 1) * LANES] = (blk * (HEAD_DIM ** -0.5)).astype(BF16)
    sk_ref[0] = _rope_block(seg(OFF_SK, 128), c64, s64, first64, 32).astype(BF16)
    sv_ref[0] = seg(OFF_SV, 128).astype(BF16)

    dq = seg(OFF_DQ, 256)
    dk = seg(OFF_DK, 256)
    dq_scale = (D_QK ** -0.5) * LOG2E
    for c in range(2):
        sl = slice(c * LANES, (c + 1) * LANES)
        dq_ref[0, :, sl] = (_rope_block(dq[:, sl], c32, s32, first32, 16) * dq_scale).astype(BF16)
        dk_ref[0, :, sl] = _rope_block(dk[:, sl], c32, s32, first32, 16).astype(BF16)
    dv = seg(OFF_DV, 512)
    for c in range(4):
        sl = slice(c * LANES, (c + 1) * LANES)
        dv_ref[0, :, sl] = jnp.where(lane == D_V, 1.0, dv[:, sl]).astype(BF16)


def _in_proj(x, g, w_packed, wg_t, tables, tm=512):
    B, S, _ = x.shape
    c64, s64, c32, s32 = tables
    row = lambda width: pl.BlockSpec((1, tm, width), lambda si, b: (b, si, 0))
    const = lambda shape: pl.BlockSpec(shape, lambda si, b: (0,) * len(shape))
    tab = pl.BlockSpec((tm, LANES), lambda si, b: (si, 0))
    out_shape = [
        jax.ShapeDtypeStruct((B, S, 512), BF16),
        jax.ShapeDtypeStruct((B, S, 256), BF16),
        jax.ShapeDtypeStruct((B, S, 256), BF16),
        jax.ShapeDtypeStruct((B, S, 128), F32),
        jax.ShapeDtypeStruct((B, 8, S), F32),
        jax.ShapeDtypeStruct((B, S, 512), BF16),
        jax.ShapeDtypeStruct((B, S, 128), BF16),
        jax.ShapeDtypeStruct((B, S, 128), BF16),
        jax.ShapeDtypeStruct((B, S, 256), BF16),
        jax.ShapeDtypeStruct((B, S, 256), BF16),
        jax.ShapeDtypeStruct((B, S, 512), BF16),
    ]
    out_specs = [row(512), row(256), row(256), row(128),
                 pl.BlockSpec((1, 8, tm), lambda si, b: (b, 0, si)),
                 row(512), row(128), row(128), row(256), row(256), row(512)]
    return pl.pallas_call(
        _in_proj_kernel,
        grid=(S // tm, B),
        in_specs=[row(D_MODEL), const((1, D_MODEL)), const((D_MODEL, PACKED_WIDTH)),
                  const((8, D_MODEL)), tab, tab, tab, tab],
        out_specs=out_specs,
        out_shape=out_shape,
        scratch_shapes=[pltpu.VMEM((tm, D_MODEL), BF16)],
        compiler_params=pltpu.CompilerParams(
            dimension_semantics=("arbitrary", "arbitrary"), vmem_limit_bytes=VMEM_LIMIT),
        name="in_proj",
    )(x, g, w_packed, wg_t, c64, s64, c32, s32)


def _mlstm_kernel(qk_ref, v_ref, o_ref, gc_ref, gr_ref, cw_ref, cb_ref, bc_ref, br_ref, ng_ref,
                  out_ref, xbuf, ct_scr, m_scr, *, T):
    L = CHUNK

    @pl.when(pl.program_id(1) == 0)
    def _():
        xbuf[0:8, :] = jnp.zeros((8, 2 * M_WIDTH), F32)
        ct_scr[...] = jnp.zeros_like(ct_scr)
        m_scr[...] = jnp.zeros_like(m_scr)

    xbuf[8:8 + T, :] = qk_ref[0].astype(F32)
    y = cb_ref[...] + sum(cw_ref[j:j + 1, :] * xbuf[5 + j:5 + j + T, :] for j in range(CONV_WIDTH))
    xbuf[0:8, :] = xbuf[T:T + 8, :]
    act = y * _sigmoid(y)
    q_all = act[:, :M_WIDTH].astype(BF16)
    k_f32 = act[:, M_WIDTH:] * (M_DIM ** -0.5)
    k_all = k_f32.astype(BF16)
    kt_all = k_f32.T.astype(BF16)

    v_f32 = v_ref[0].astype(F32)
    gc = gc_ref[0] + bc_ref[...]
    gr = gr_ref[0] + br_ref[...]
    logf_c = _log_sigmoid(gc)
    logf_r = _log_sigmoid(gr)

    ri = lax.broadcasted_iota(jnp.int32, (L, L), 0)
    ci = lax.broadcasted_iota(jnp.int32, (L, L), 1)
    causal = ci <= ri
    tri = jnp.where(causal, 1.0, 0.0).astype(BF16)
    tri_t = jnp.where(ri <= ci, 1.0, 0.0).astype(BF16)
    lane = lax.broadcasted_iota(jnp.int32, (L, LANES), 1)

    for c in range(T // L):
        r0 = c * L
        cum_c = sum(_dot(tri, p) for p in _split3(logf_c[r0:r0 + L, :]))
        cum_r = sum(_dot(p, tri_t) for p in _split3(logf_r[:, r0:r0 + L]))
        ig_c = gc[r0:r0 + L, :]
        ig_r = gr[:, r0:r0 + L]
        heads = []
        for h in range(M_HEADS):
            b_col = cum_c[:, 4 + h:5 + h]
            b_row = cum_r[4 + h:5 + h, :]
            i_row = ig_r[h:h + 1, :]
            i_col = ig_c[:, h:h + 1]
            m_prev = m_scr[h, 0:1, 0:1]

            q_h = q_all[r0:r0 + L, h * M_DIM:(h + 1) * M_DIM]
            k_h = k_all[r0:r0 + L, h * M_DIM:(h + 1) * M_DIM]
            kt_h = kt_all[h * M_DIM:(h + 1) * M_DIM, r0:r0 + L]
            vblk = v_f32[r0:r0 + L, (h // 2) * LANES:(h // 2 + 1) * LANES]
            if h % 2:
                vblk = pltpu.roll(vblk, M_DIM, 1)
            v_ext = jnp.where(lane < M_DIM, vblk, jnp.where(lane == M_DIM, 1.0, 0.0))

            dmat = jnp.where(causal, b_col - b_row + i_row, -jnp.inf)
            m_t = jnp.maximum(b_col + m_prev, jnp.max(dmat, axis=1, keepdims=True))
            w_intra = jnp.exp(dmat - m_t)
            sc = (_nt_dot(q_h, k_h) * w_intra).astype(BF16)
            a_inter = jnp.exp(b_col + m_prev - m_t)
            ct = ct_scr[h]
            num = a_inter * _dot(q_h, ct.astype(BF16)) + _dot(sc, v_ext.astype(BF16))
            den = num[:, M_DIM:M_DIM + 1]
            hh = num[:, :M_DIM] / jnp.maximum(jnp.abs(den), jnp.exp(-m_t))

            b_last = b_col[L - 1:L, :]
            g_col = b_last - b_col + i_col
            m_new = jnp.maximum(b_last + m_prev, jnp.max(g_col, axis=0, keepdims=True))
            w_state = jnp.exp(g_col - m_new)
            decay = jnp.exp(b_last + m_prev - m_new)
            ct_scr[h] = decay * ct + _dot(kt_h, (w_state * v_ext).astype(BF16))
            m_scr[h] = jnp.broadcast_to(m_new, (8, LANES))

            heads.append(_rms(hh, ng_ref[:, h * M_DIM:(h + 1) * M_DIM]))
        hcat = jnp.concatenate(heads, axis=-1)
        gate = _sigmoid(o_ref[0, r0:r0 + L, :].astype(F32))
        out_ref[0, r0:r0 + L, :] = (gate * hcat).astype(BF16)


def _mlstm(qk, mv, mo, gate_c, gate_r, conv_w, conv_b, bias_c, bias_r, norm_g, T=256):
    B, S, _ = qk.shape
    row = lambda width: pl.BlockSpec((1, T, width), lambda b, t: (b, t, 0))
    const = lambda shape: pl.BlockSpec(shape, lambda b, t: (0,) * len(shape))
    return pl.pallas_call(
        functools.partial(_mlstm_kernel, T=T),
        grid=(B, S // T),
        in_specs=[row(512), row(256), row(256), row(128),
                  pl.BlockSpec((1, 8, T), lambda b, t: (b, 0, t)),
                  const((CONV_WIDTH, 2 * M_WIDTH)), const((1, 2 * M_WIDTH)),
                  const((1, LANES)), const((8, 1)), const((1, M_WIDTH))],
        out_specs=row(M_WIDTH),
        out_shape=jax.ShapeDtypeStruct((B, S, M_WIDTH), BF16),
        scratch_shapes=[pltpu.VMEM((T + 8, 2 * M_WIDTH), F32),
                        pltpu.VMEM((M_HEADS, M_DIM, LANES), F32),
                        pltpu.VMEM((M_HEADS, 8, LANES), F32)],
        compiler_params=pltpu.CompilerParams(
            dimension_semantics=("arbitrary", "arbitrary"), vmem_limit_bytes=VMEM_LIMIT),
        name="mlstm",
    )(qk, mv, mo, gate_c, gate_r, conv_w, conv_b, bias_c, bias_r, norm_g)


def _swa_kernel(sink_ref, q_ref, kc_ref, vc_ref, kp_ref, vp_ref, out_ref, *, TQ):
    W = WINDOW
    G = S_HEADS // S_KV_HEADS
    tile = pl.program_id(1)
    r = lax.broadcasted_iota(jnp.int32, (G * W, 2 * W), 0) % W
    c = lax.broadcasted_iota(jnp.int32, (G * W, 2 * W), 1)
    band = (c > r) & (c <= r + W)
    for j in range(TQ // W):
        if j == 0:
            kband = jnp.concatenate([kp_ref[0], kc_ref[0, 0:W, :]], axis=0)
            vband = jnp.concatenate([vp_ref[0], vc_ref[0, 0:W, :]], axis=0)
            valid = band & ((tile > 0) | (c >= W))
        else:
            kband = kc_ref[0, (j - 1) * W:(j + 1) * W, :]
            vband = vc_ref[0, (j - 1) * W:(j + 1) * W, :]
            valid = band
        outs = []
        for hk in range(S_KV_HEADS):
            kb = kband[:, hk * HEAD_DIM:(hk + 1) * HEAD_DIM]
            vb = vband[:, hk * HEAD_DIM:(hk + 1) * HEAD_DIM]
            qs = jnp.concatenate(
                [q_ref[0, j * W:(j + 1) * W, (hk * G + g) * HEAD_DIM:(hk * G + g + 1) * HEAD_DIM]
                 for g in range(G)], axis=0)
            sink = jnp.concatenate(
                [jnp.full((W, 1), sink_ref[hk * G + g], F32) for g in range(G)], axis=0)
            s = jnp.where(valid, _nt_dot(qs, kb), -jnp.inf)
            mx = jnp.maximum(jnp.max(s, axis=-1, keepdims=True), sink)
            e = jnp.exp(s - mx)
            denom = jnp.sum(e, axis=-1, keepdims=True) + jnp.exp(sink - mx)
            o = _dot((e / denom).astype(BF16), vb)
            outs.extend(o[g * W:(g + 1) * W, :] for g in range(G))
        out_ref[0, j * W:(j + 1) * W, :] = jnp.concatenate(outs, axis=-1).astype(BF16)


def _swa(sq, sk, sv, sinks, TQ=256):
    B, S, _ = sq.shape
    per = TQ // WINDOW
    cur = lambda width: pl.BlockSpec((1, TQ, width), lambda b, t: (b, t, 0))
    prev = pl.BlockSpec((1, WINDOW, S_KV_WIDTH), lambda b, t: (b, jnp.maximum(t * per - 1, 0), 0))
    return pl.pallas_call(
        functools.partial(_swa_kernel, TQ=TQ),
        grid=(B, S // TQ),
        in_specs=[pl.BlockSpec(memory_space=pltpu.SMEM),
                  cur(S_WIDTH), cur(S_KV_WIDTH), cur(S_KV_WIDTH), prev, prev],
        out_specs=cur(S_WIDTH),
        out_shape=jax.ShapeDtypeStruct((B, S, S_WIDTH), BF16),
        compiler_params=pltpu.CompilerParams(
            dimension_semantics=("arbitrary", "arbitrary"), vmem_limit_bytes=VMEM_LIMIT),
        name="swa",
    )(sinks, sq, sk, sv, sk, sv)


def _diff_kernel(lamp_ref, subg_ref, q_ref, k_ref, v_ref, out_ref, m_scr, acc_scr, *, TQ, TK, lam_init):
    qi = pl.program_id(2)
    q = q_ref[0]
    lane = lax.broadcasted_iota(jnp.int32, (TQ, LANES), 1)
    zero = jnp.zeros_like(q)
    qg = [jnp.where((lane >= g * D_QK) & (lane < (g + 1) * D_QK), q, zero) for g in range(4)]

    m_scr[...] = jnp.full_like(m_scr, -jnp.inf)
    acc_scr[...] = jnp.zeros_like(acc_scr)

    row_pos = qi * TQ + lax.broadcasted_iota(jnp.int32, (TQ, TK), 0)
    col_in = lax.broadcasted_iota(jnp.int32, (TQ, TK), 1)

    def step(j, masked):
        start = pl.multiple_of(j * TK, TK)
        kb = k_ref[0, pl.ds(start, TK), :]
        vb = v_ref[0, pl.ds(start, TK), :]
        if masked:
            keep = (start + col_in) <= row_pos
        for g in range(4):
            s = _nt_dot(qg[g], kb)
            if masked:
                s = jnp.where(keep, s, -jnp.inf)
            m_old = m_scr[g]
            m_new = jnp.maximum(m_old, jnp.max(s, axis=-1, keepdims=True))
            p = jnp.exp2(s - m_new).astype(BF16)
            vh = vb[:, (g // 2) * LANES:(g // 2 + 1) * LANES]
            acc_scr[g] = jnp.exp2(m_old - m_new) * acc_scr[g] + _dot(p, vh)
            m_scr[g] = m_new

    n_full = (qi * TQ) // TK
    n_all = ((qi + 1) * TQ + TK - 1) // TK

    def full_body(j, carry):
        step(j, False)
        return carry

    def diag_body(j, carry):
        step(j, True)
        return carry

    lax.fori_loop(0, n_full, full_body, 0)
    lax.fori_loop(n_full, n_all, diag_body, 0)

    lamp = lamp_ref[...]
    lam = (jnp.exp(jnp.sum(lamp[0:1] * lamp[1:2], axis=-1, keepdims=True))
           - jnp.exp(jnp.sum(lamp[2:3] * lamp[3:4], axis=-1, keepdims=True)) + lam_init)
    outs = []
    for hd in range(2):
        a1 = acc_scr[2 * hd]
        a2 = acc_scr[2 * hd + 1]
        o = a1[:, :D_V] / a1[:, D_V:D_V + 1] - lam * (a2[:, :D_V] / a2[:, D_V:D_V + 1])
        outs.append(_rms(o, subg_ref[...]) * (1.0 - lam_init))
    out_ref[0] = jnp.concatenate(outs, axis=-1).astype(BF16)


def _diffattn(dq, dk, dvx, lamp, sub_g, lam_init, TQ=256, TK=512):
    B, S, _ = dq.shape
    return pl.pallas_call(
        functools.partial(_diff_kernel, TQ=TQ, TK=TK, lam_init=lam_init),
        grid=(B, D_HEADS // 2, S // TQ),
        in_specs=[pl.BlockSpec((4, LANES), lambda b, hp, i: (0, 0)),
                  pl.BlockSpec((1, D_V), lambda b, hp, i: (0, 0)),
                  pl.BlockSpec((1, TQ, LANES), lambda b, hp, i: (b, i, hp)),
                  pl.BlockSpec((1, S, LANES), lambda b, hp, i: (b, 0, hp)),
                  pl.BlockSpec((1, S, 2 * LANES), lambda b, hp, i: (b, 0, hp))],
        out_specs=pl.BlockSpec((1, TQ, LANES), lambda b, hp, i: (b, i, hp)),
        out_shape=jax.ShapeDtypeStruct((B, S, D_WIDTH), BF16),
        scratch_shapes=[pltpu.VMEM((4, TQ, 1), F32), pltpu.VMEM((4, TQ, LANES), F32)],
        compiler_params=pltpu.CompilerParams(
            dimension_semantics=("arbitrary", "arbitrary", "arbitrary"), vmem_limit_bytes=VMEM_LIMIT),
        name="diffattn",
    )(lamp, sub_g, dq, dk, dvx)


def _out_mlp_kernel(x_ref, om_ref, os_ref, od_ref, wo_ref, wup_ref, wdn_ref,
                    gpm_ref, gpre_ref, gpost_ref, out_ref, *, FC):
    mix = (_dot(om_ref[...], wo_ref[0:M_WIDTH, :])
           + _dot(os_ref[...], wo_ref[M_WIDTH:M_WIDTH + S_WIDTH, :])
           + _dot(od_ref[...], wo_ref[M_WIDTH + S_WIDTH:, :]))
    x1 = x_ref[...] + _rms(mix, gpm_ref[...])
    h2 = _rms(x1, gpre_ref[...]).astype(BF16)
    acc = jnp.zeros(x1.shape, F32)
    for c in range(D_FF // FC):
        u = jnp.maximum(_dot(h2, wup_ref[:, c * FC:(c + 1) * FC]), 0.0)
        acc = acc + _dot((u * u).astype(BF16), wdn_ref[c * FC:(c + 1) * FC, :])
    out_ref[...] = x1 + _rms(acc, gpost_ref[...])


def _out_mlp(x2d, om, os_, od, w_out, w_up, w_down, g_post_mix, g_pre_mlp, g_post_mlp, tm=512, FC=1024):
    N = x2d.shape[0]
    row = lambda width: pl.BlockSpec((tm, width), lambda i: (i, 0))
    const = lambda shape: pl.BlockSpec(shape, lambda i: (0, 0), pipeline_mode=pl.Buffered(1))
    return pl.pallas_call(
        functools.partial(_out_mlp_kernel, FC=FC),
        grid=(N // tm,),
        in_specs=[row(D_MODEL), row(M_WIDTH), row(S_WIDTH), row(D_WIDTH),
                  const((D_MODEL, D_MODEL)), const((D_MODEL, D_FF)), const((D_FF, D_MODEL)),
                  const((1, D_MODEL)), const((1, D_MODEL)), const((1, D_MODEL))],
        out_specs=row(D_MODEL),
        out_shape=jax.ShapeDtypeStruct((N, D_MODEL), F32),
        compiler_params=pltpu.CompilerParams(
            dimension_semantics=("arbitrary",), vmem_limit_bytes=VMEM_LIMIT),
        name="out_mlp",
    )(x2d, om, os_, od, w_out, w_up, w_down, g_post_mix, g_pre_mlp, g_post_mlp)


def _rope_tables(S):
    pos = jnp.arange(S, dtype=F32)[:, None]
    lane = np.arange(LANES)

    def table(head_dim):
        half = head_dim // 2
        inv = ROPE_THETA ** (-jnp.arange(half, dtype=F32) * 2.0 / head_dim)
        ang = pos * inv[None, :]
        idx = (lane % head_dim) % half
        sign = np.where((lane % head_dim) < half, -1.0, 1.0).astype(np.float32)
        return jnp.cos(ang)[:, idx], jnp.sin(ang)[:, idx] * sign

    c64, s64 = table(HEAD_DIM)
    c32, s32 = table(D_QK)
    return c64, s64, c32, s32


def _pack_w_in(w_in):
    sizes = (M_WIDTH, M_WIDTH, M_WIDTH, M_WIDTH, M_HEADS, M_HEADS,
             S_WIDTH, S_KV_WIDTH, S_KV_WIDTH, D_QK_WIDTH, D_QK_WIDTH, D_WIDTH)
    offs = np.concatenate([[0], np.cumsum(sizes)])
    part = [w_in[:, offs[i]:offs[i + 1]] for i in range(len(sizes))]
    mq, mk, mv, mo, mi, mf, sq, sk, sv, dq, dk, dv = part
    zeros = lambda n: jnp.zeros((D_MODEL, n), w_in.dtype)
    gates = jnp.concatenate([mi, mf, zeros(LANES - 2 * M_HEADS)], axis=1)
    dv_blocks = []
    for h in range(D_HEADS):
        dv_blocks += [dv[:, h * D_V:(h + 1) * D_V], zeros(LANES - D_V)]
    packed = jnp.concatenate([mq, mk, mv, mo, gates, sq, sk, sv, dq, dk] + dv_blocks, axis=1)
    wg_t = jnp.concatenate([mi, mf], axis=1).T
    return packed.astype(BF16), wg_t.astype(BF16)


def kernel(x, w_in, conv_w, conv_b, i_bias, f_bias, m_norm_g, sinks, lam_q1, lam_k1, lam_q2, lam_k2,
           sub_g, w_out, w_up, w_down, g_pre_mix, g_post_mix, g_pre_mlp, g_post_mlp):
    B, S, D = x.shape
    depth = w_in.shape[0]
    tables = _rope_tables(S)
    pad_lanes = lambda v: jnp.pad(v, (0, LANES - v.shape[0]))[None, :]
    for l in range(depth):
        w_packed, wg_t = _pack_w_in(w_in[l])
        (qk, mv, mo, gate_c, gate_r, sq, sk, sv, dq, dk, dvx) = _in_proj(
            x, g_pre_mix[l][None, :], w_packed, wg_t, tables)

        gate_bias = jnp.concatenate([i_bias[l], f_bias[l]])
        out_m = _mlstm(qk, mv, mo, gate_c, gate_r, conv_w[l], conv_b[l][None, :],
                       pad_lanes(gate_bias), gate_bias[:, None], m_norm_g[l][None, :])
        out_s = _swa(sq, sk, sv, sinks[l])
        lamp = jnp.concatenate([pad_lanes(v) for v in (lam_q1[l], lam_k1[l], lam_q2[l], lam_k2[l])], axis=0)
        lam_init = 0.8 - 0.6 * math.exp(-0.3 * l)
        out_d = _diffattn(dq, dk, dvx, lamp, sub_g[l][None, :], lam_init)

        x = _out_mlp(x.reshape(B * S, D), out_m.reshape(B * S, -1), out_s.reshape(B * S, -1),
                     out_d.reshape(B * S, -1), w_out[l].astype(BF16), w_up[l].astype(BF16),
                     w_down[l].astype(BF16), g_post_mix[l][None, :], g_pre_mlp[l][None, :],
                     g_post_mlp[l][None, :]).reshape(B, S, D)
    return x
```

```python
import functools
import math

import jax
import jax.numpy as jnp
import numpy as np
from jax import lax
from jax.experimental import pallas as pl
from jax.experimental.pallas import tpu as pltpu

D_MODEL = 1024
M_HEADS = 4
M_DIM = 64
CONV_WIDTH = 4
CHUNK = 128
S_HEADS = 8
S_KV_HEADS = 2
HEAD_DIM = 64
WINDOW = 128
D_HEADS = 4
D_QK = 32
D_V = 64
ROPE_THETA = 10000.0
D_FF = 4 * D_MODEL
EPS = 1e-6

M_WIDTH = M_HEADS * M_DIM
S_WIDTH = S_HEADS * HEAD_DIM
S_KV_WIDTH = S_KV_HEADS * HEAD_DIM
D_QK_WIDTH = D_HEADS * 2 * D_QK
D_WIDTH = D_HEADS * D_V

LANES = 128
VMEM_LIMIT = 56 * 1024 * 1024

LOG2E = 1.4426950408889634

BF16 = jnp.bfloat16
F32 = jnp.float32

OFF_QK = 0
OFF_MV = 512
OFF_MO = 768
OFF_G = 1024
OFF_SQ = 1152
OFF_SK = 1664
OFF_SV = 1792
OFF_DQ = 1920
OFF_DK = 2176
PACKED_WIDTH = 2432


def _nt_dot(a, b):
    return lax.dot_general(a, b, (((1,), (1,)), ((), ())), preferred_element_type=F32)


def _dot(a, b):
    return jnp.dot(a, b, preferred_element_type=F32)


def _rms(x, g):
    return x * lax.rsqrt(jnp.mean(x * x, axis=-1, keepdims=True) + EPS) * g


def _sigmoid(x):
    return 1.0 / (1.0 + jnp.exp(-x))


def _log_sigmoid(x):
    return jnp.minimum(x, 0.0) - jnp.log(1.0 + jnp.exp(-jnp.abs(x)))


def _split3(x):
    hi = x.astype(BF16)
    r1 = x - hi.astype(F32)
    mid = r1.astype(BF16)
    lo = (r1 - mid.astype(F32)).astype(BF16)
    return hi, mid, lo


def _rope_block(zb, cos, sin_signed, first_half, half):
    fwd = pltpu.roll(zb, LANES - half, 1)
    bwd = pltpu.roll(zb, half, 1)
    return zb * cos + jnp.where(first_half, fwd, bwd) * sin_signed


def _in_proj_kernel(x_ref, g_ref, w_ref, wgt_ref, wdvt_ref, c64_ref, s64_ref, c32_ref, s32_ref,
                    qk_ref, mv_ref, mo_ref, gate_ref, gatet_ref,
                    sq_ref, sk_ref, sv_ref, dq_ref, dk_ref, dvt_ref, h_scr):
    xf = x_ref[0]
    h_scr[...] = _rms(xf, g_ref[...]).astype(BF16)
    h = h_scr[...]

    def seg(off, width):
        return _dot(h, w_ref[:, off:off + width])

    qk_ref[0] = seg(OFF_QK, 512).astype(BF16)
    mv_ref[0] = seg(OFF_MV, 256).astype(BF16)
    mo_ref[0] = seg(OFF_MO, 256).astype(BF16)
    gate_ref[0] = seg(OFF_G, 128)
    gatet_ref[0] = _nt_dot(wgt_ref[...], h)

    lane = lax.broadcasted_iota(jnp.int32, (xf.shape[0], LANES), 1)
    first64 = (lane % 64) < 32
    first32 = (lane % 32) < 16
    c64, s64 = c64_ref[...], s64_ref[...]
    c32, s32 = c32_ref[...], s32_ref[...]

    sq = seg(OFF_SQ, 512)
    for c in range(4):
        blk = _rope_block(sq[:, c * LANES:(c + 1) * LANES], c64, s64, first64, 32)
        sq_ref[0, :, c * LANES:(c + 1) * LANES] = (blk * (HEAD_DIM ** -0.5)).astype(BF16)
    sk_ref[0] = _rope_block(seg(OFF_SK, 128), c64, s64, first64, 32).astype(BF16)
    sv_ref[0] = seg(OFF_SV, 128).astype(BF16)

    dq = seg(OFF_DQ, 256)
    dk = seg(OFF_DK, 256)
    dq_scale = (D_QK ** -0.5) * LOG2E
    for c in range(2):
        sl = slice(c * LANES, (c + 1) * LANES)
        dq_ref[0, :, sl] = (_rope_block(dq[:, sl], c32, s32, first32, 16) * dq_scale).astype(BF16)
        dk_ref[0, :, sl] = _rope_block(dk[:, sl], c32, s32, first32, 16).astype(BF16)
    dvt = _nt_dot(wdvt_ref[...], h)
    feat = lax.broadcasted_iota(jnp.int32, dvt.shape, 0) % LANES
    dvt_ref[0] = jnp.where(feat == D_V, 1.0, dvt).astype(BF16)


def _in_proj(x, g, w_packed, wg_t, wdv_t, tables, tm=512):
    B, S, _ = x.shape
    c64, s64, c32, s32 = tables
    row = lambda width: pl.BlockSpec((1, tm, width), lambda si, b: (b, si, 0))
    const = lambda shape: pl.BlockSpec(shape, lambda si, b: (0,) * len(shape))
    tab = pl.BlockSpec((tm, LANES), lambda si, b: (si, 0))
    out_shape = [
        jax.ShapeDtypeStruct((B, S, 512), BF16),
        jax.ShapeDtypeStruct((B, S, 256), BF16),
        jax.ShapeDtypeStruct((B, S, 256), BF16),
        jax.ShapeDtypeStruct((B, S, 128), F32),
        jax.ShapeDtypeStruct((B, 8, S), F32),
        jax.ShapeDtypeStruct((B, S, 512), BF16),
        jax.ShapeDtypeStruct((B, S, 128), BF16),
        jax.ShapeDtypeStruct((B, S, 128), BF16),
        jax.ShapeDtypeStruct((B, S, 256), BF16),
        jax.ShapeDtypeStruct((B, S, 256), BF16),
        jax.ShapeDtypeStruct((B, 512, S), BF16),
    ]
    out_specs = [row(512), row(256), row(256), row(128),
                 pl.BlockSpec((1, 8, tm), lambda si, b: (b, 0, si)),
                 row(512), row(128), row(128), row(256), row(256),
                 pl.BlockSpec((1, 512, tm), lambda si, b: (b, 0, si))]
    return pl.pallas_call(
        _in_proj_kernel,
        grid=(S // tm, B),
        in_specs=[row(D_MODEL), const((1, D_MODEL)), const((D_MODEL, PACKED_WIDTH)),
                  const((8, D_MODEL)), const((512, D_MODEL)), tab, tab, tab, tab],
        out_specs=out_specs,
        out_shape=out_shape,
        scratch_shapes=[pltpu.VMEM((tm, D_MODEL), BF16)],
        compiler_params=pltpu.CompilerParams(
            dimension_semantics=("arbitrary", "arbitrary"), vmem_limit_bytes=VMEM_LIMIT),
        name="in_proj",
    )(x, g, w_packed, wg_t, wdv_t, c64, s64, c32, s32)


def _mlstm_kernel(qk_ref, v_ref, o_ref, gc_ref, gr_ref, cw_ref, cb_ref, bc_ref, br_ref, ng_ref,
                  out_ref, xbuf, ct_scr, m_scr, *, T):
    L = CHUNK

    @pl.when(pl.program_id(1) == 0)
    def _():
        xbuf[0:8, :] = jnp.zeros((8, 2 * M_WIDTH), F32)
        ct_scr[...] = jnp.zeros_like(ct_scr)
        m_scr[...] = jnp.zeros_like(m_scr)

    xbuf[8:8 + T, :] = qk_ref[0].astype(F32)
    y = cb_ref[...] + sum(cw_ref[j:j + 1, :] * xbuf[5 + j:5 + j + T, :] for j in range(CONV_WIDTH))
    xbuf[0:8, :] = xbuf[T:T + 8, :]
    act = y * _sigmoid(y)
    q_all = act[:, :M_WIDTH].astype(BF16)
    k_f32 = act[:, M_WIDTH:] * (M_DIM ** -0.5)
    k_all = k_f32.astype(BF16)
    kt_all = k_f32.T.astype(BF16)

    v_f32 = v_ref[0].astype(F32)
    gc = gc_ref[0] + bc_ref[...]
    gr = gr_ref[0] + br_ref[...]
    logf_c = _log_sigmoid(gc)
    logf_r = _log_sigmoid(gr)

    ri = lax.broadcasted_iota(jnp.int32, (L, L), 0)
    ci = lax.broadcasted_iota(jnp.int32, (L, L), 1)
    causal = ci <= ri
    tri = jnp.where(causal, 1.0, 0.0).astype(BF16)
    tri_t = jnp.where(ri <= ci, 1.0, 0.0).astype(BF16)
    lane = lax.broadcasted_iota(jnp.int32, (L, LANES), 1)

    for c in range(T // L):
        r0 = c * L
        cum_c = sum(_dot(tri, p) for p in _split3(logf_c[r0:r0 + L, :]))
        cum_r = sum(_dot(p, tri_t) for p in _split3(logf_r[:, r0:r0 + L]))
        ig_c = gc[r0:r0 + L, :]
        ig_r = gr[:, r0:r0 + L]
        heads = []
        for h in range(M_HEADS):
            b_col = cum_c[:, 4 + h:5 + h]
            b_row = cum_r[4 + h:5 + h, :]
            i_row = ig_r[h:h + 1, :]
            i_col = ig_c[:, h:h + 1]
            m_prev = m_scr[h, 0:1, 0:1]

            q_h = q_all[r0:r0 + L, h * M_DIM:(h + 1) * M_DIM]
            k_h = k_all[r0:r0 + L, h * M_DIM:(h + 1) * M_DIM]
            kt_h = kt_all[h * M_DIM:(h + 1) * M_DIM, r0:r0 + L]
            vblk = v_f32[r0:r0 + L, (h // 2) * LANES:(h // 2 + 1) * LANES]
            if h % 2:
                vblk = pltpu.roll(vblk, M_DIM, 1)
            v_ext = jnp.where(lane < M_DIM, vblk, jnp.where(lane == M_DIM, 1.0, 0.0))

            dmat = jnp.where(causal, b_col - b_row + i_row, -jnp.inf)
            m_t = jnp.maximum(b_col + m_prev, jnp.max(dmat, axis=1, keepdims=True))
            w_intra = jnp.exp(dmat - m_t)
            sc = (_nt_dot(q_h, k_h) * w_intra).astype(BF16)
            a_inter = jnp.exp(b_col + m_prev - m_t)
            ct = ct_scr[h]
            num = a_inter * _dot(q_h, ct.astype(BF16)) + _dot(sc, v_ext.astype(BF16))
            den = num[:, M_DIM:M_DIM + 1]
            hh = num[:, :M_DIM] / jnp.maximum(jnp.abs(den), jnp.exp(-m_t))

            b_last = b_col[L - 1:L, :]
            g_col = b_last - b_col + i_col
            m_new = jnp.maximum(b_last + m_prev, jnp.max(g_col, axis=0, keepdims=True))
            w_state = jnp.exp(g_col - m_new)
            decay = jnp.exp(b_last + m_prev - m_new)
            ct_scr[h] = decay * ct + _dot(kt_h, (w_state * v_ext).astype(BF16))
            m_scr[h] = jnp.broadcast_to(m_new, (8, LANES))

            heads.append(_rms(hh, ng_ref[:, h * M_DIM:(h + 1) * M_DIM]))
        hcat = jnp.concatenate(heads, axis=-1)
        gate = _sigmoid(o_ref[0, r0:r0 + L, :].astype(F32))
        out_ref[0, r0:r0 + L, :] = (gate * hcat).astype(BF16)


def _mlstm(qk, mv, mo, gate_c, gate_r, conv_w, conv_b, bias_c, bias_r, norm_g, T=256):
    B, S, _ = qk.shape
    row = lambda width: pl.BlockSpec((1, T, width), lambda b, t: (b, t, 0))
    const = lambda shape: pl.BlockSpec(shape, lambda b, t: (0,) * len(shape))
    return pl.pallas_call(
        functools.partial(_mlstm_kernel, T=T),
        grid=(B, S // T),
        in_specs=[row(512), row(256), row(256), row(128),
                  pl.BlockSpec((1, 8, T), lambda b, t: (b, 0, t)),
                  const((CONV_WIDTH, 2 * M_WIDTH)), const((1, 2 * M_WIDTH)),
                  const((1, LANES)), const((8, 1)), const((1, M_WIDTH))],
        out_specs=row(M_WIDTH),
        out_shape=jax.ShapeDtypeStruct((B, S, M_WIDTH), BF16),
        scratch_shapes=[pltpu.VMEM((T + 8, 2 * M_WIDTH), F32),
                        pltpu.VMEM((M_HEADS, M_DIM, LANES), F32),
                        pltpu.VMEM((M_HEADS, 8, LANES), F32)],
        compiler_params=pltpu.CompilerParams(
            dimension_semantics=("arbitrary", "arbitrary"), vmem_limit_bytes=VMEM_LIMIT),
        name="mlstm",
    )(qk, mv, mo, gate_c, gate_r, conv_w, conv_b, bias_c, bias_r, norm_g)


def _swa_kernel(sink_ref, q_ref, kc_ref, vc_ref, kp_ref, vp_ref, out_ref, *, TQ):
    W = WINDOW
    G = S_HEADS // S_KV_HEADS
    tile = pl.program_id(1)
    r = lax.broadcasted_iota(jnp.int32, (G * W, 2 * W), 0) % W
    c = lax.broadcasted_iota(jnp.int32, (G * W, 2 * W), 1)
    band = (c > r) & (c <= r + W)
    for j in range(TQ // W):
        if j == 0:
            kband = jnp.concatenate([kp_ref[0], kc_ref[0, 0:W, :]], axis=0)
            vband = jnp.concatenate([vp_ref[0], vc_ref[0, 0:W, :]], axis=0)
            valid = band & ((tile > 0) | (c >= W))
        else:
            kband = kc_ref[0, (j - 1) * W:(j + 1) * W, :]
            vband = vc_ref[0, (j - 1) * W:(j + 1) * W, :]
            valid = band
        outs = []
        for hk in range(S_KV_HEADS):
            kb = kband[:, hk * HEAD_DIM:(hk + 1) * HEAD_DIM]
            vb = vband[:, hk * HEAD_DIM:(hk + 1) * HEAD_DIM]
            qs = jnp.concatenate(
                [q_ref[0, j * W:(j + 1) * W, (hk * G + g) * HEAD_DIM:(hk * G + g + 1) * HEAD_DIM]
                 for g in range(G)], axis=0)
            sink = jnp.concatenate(
                [jnp.full((W, 1), sink_ref[hk * G + g], F32) for g in range(G)], axis=0)
            s = jnp.where(valid, _nt_dot(qs, kb), -jnp.inf)
            mx = jnp.maximum(jnp.max(s, axis=-1, keepdims=True), sink)
            e = jnp.exp(s - mx)
            denom = jnp.sum(e, axis=-1, keepdims=True) + jnp.exp(sink - mx)
            o = _dot((e / denom).astype(BF16), vb)
            outs.extend(o[g * W:(g + 1) * W, :] for g in range(G))
        out_ref[0, j * W:(j + 1) * W, :] = jnp.concatenate(outs, axis=-1).astype(BF16)


def _swa(sq, sk, sv, sinks, TQ=256):
    B, S, _ = sq.shape
    per = TQ // WINDOW
    cur = lambda width: pl.BlockSpec((1, TQ, width), lambda b, t: (b, t, 0))
    prev = pl.BlockSpec((1, WINDOW, S_KV_WIDTH), lambda b, t: (b, jnp.maximum(t * per - 1, 0), 0))
    return pl.pallas_call(
        functools.partial(_swa_kernel, TQ=TQ),
        grid=(B, S // TQ),
        in_specs=[pl.BlockSpec(memory_space=pltpu.SMEM),
                  cur(S_WIDTH), cur(S_KV_WIDTH), cur(S_KV_WIDTH), prev, prev],
        out_specs=cur(S_WIDTH),
        out_shape=jax.ShapeDtypeStruct((B, S, S_WIDTH), BF16),
        compiler_params=pltpu.CompilerParams(
            dimension_semantics=("arbitrary", "arbitrary"), vmem_limit_bytes=VMEM_LIMIT),
        name="swa",
    )(sinks, sq, sk, sv, sk, sv)


def _diff_kernel(lamp_ref, subg_ref, q_ref, k_ref, vt_ref, out_ref, s_scr, m_scr, acc_scr, *, TQ, TK, lam_init):
    qi = pl.program_id(2)
    q = q_ref[0]
    lane = lax.broadcasted_iota(jnp.int32, (TQ, LANES), 1)
    zero = jnp.zeros_like(q)
    qg = [jnp.where((lane >= g * D_QK) & (lane < (g + 1) * D_QK), q, zero) for g in range(4)]

    m_scr[...] = jnp.full_like(m_scr, -jnp.inf)
    acc_scr[...] = jnp.zeros_like(acc_scr)

    key_in = lax.broadcasted_iota(jnp.int32, (TK, TQ), 0)
    q_pos = qi * TQ + lax.broadcasted_iota(jnp.int32, (TK, TQ), 1)

    def scores(j, g):
        start = pl.multiple_of(j * TK, TK)
        s_scr[g] = _nt_dot(k_ref[0, pl.ds(start, TK), :], qg[g])

    def softmax_pv(j, g, masked):
        start = pl.multiple_of(j * TK, TK)
        s = s_scr[g]
        if masked:
            s = jnp.where((start + key_in) <= q_pos, s, -jnp.inf)
        m_old = m_scr[g]
        m_new = jnp.maximum(m_old, jnp.max(s, axis=0, keepdims=True))
        p = jnp.exp2(s - m_new[0:1, :]).astype(BF16)
        alpha = jnp.exp2(m_old[0:1, :] - m_new[0:1, :])
        vt = vt_ref[0, (g // 2) * LANES:(g // 2 + 1) * LANES, pl.ds(start, TK)]
        acc_scr[g] = alpha * acc_scr[g] + _dot(vt, p)
        m_scr[g] = m_new

    def pipelined(j, masked, prefetch_next):
        scores(j, 2)
        softmax_pv(j, 0, masked)
        scores(j, 3)
        softmax_pv(j, 1, masked)
        if prefetch_next:
            scores(j + 1, 0)
        softmax_pv(j, 2, masked)
        if prefetch_next:
            scores(j + 1, 1)
        softmax_pv(j, 3, masked)

    n_full = (qi * TQ) // TK
    scores(0, 0)
    scores(0, 1)

    def full_body(j, carry):
        pipelined(j, False, True)
        return carry

    lax.fori_loop(0, n_full, full_body, 0)
    pipelined(n_full, True, False)

    lamp = lamp_ref[...]
    lam = (jnp.exp(jnp.sum(lamp[0:1] * lamp[1:2], axis=-1, keepdims=True))
           - jnp.exp(jnp.sum(lamp[2:3] * lamp[3:4], axis=-1, keepdims=True)) + lam_init)
    outs = []
    for hd in range(2):
        a1 = acc_scr[2 * hd].T
        a2 = acc_scr[2 * hd + 1].T
        o = a1[:, :D_V] / a1[:, D_V:D_V + 1] - lam * (a2[:, :D_V] / a2[:, D_V:D_V + 1])
        outs.append(_rms(o, subg_ref[...]) * (1.0 - lam_init))
    out_ref[0] = jnp.concatenate(outs, axis=-1).astype(BF16)


def _diffattn(dq, dk, dvt, lamp, sub_g, lam_init, TQ=256, TK=512):
    B, S, _ = dq.shape
    return pl.pallas_call(
        functools.partial(_diff_kernel, TQ=TQ, TK=TK, lam_init=lam_init),
        grid=(B, D_HEADS // 2, S // TQ),
        in_specs=[pl.BlockSpec((4, LANES), lambda b, hp, i: (0, 0)),
                  pl.BlockSpec((1, D_V), lambda b, hp, i: (0, 0)),
                  pl.BlockSpec((1, TQ, LANES), lambda b, hp, i: (b, i, hp)),
                  pl.BlockSpec((1, S, LANES), lambda b, hp, i: (b, 0, hp)),
                  pl.BlockSpec((1, 2 * LANES, S), lambda b, hp, i: (b, hp, 0))],
        out_specs=pl.BlockSpec((1, TQ, LANES), lambda b, hp, i: (b, i, hp)),
        out_shape=jax.ShapeDtypeStruct((B, S, D_WIDTH), BF16),
        scratch_shapes=[pltpu.VMEM((4, TK, TQ), F32), pltpu.VMEM((4, 8, TQ), F32),
                        pltpu.VMEM((4, LANES, TQ), F32)],
        compiler_params=pltpu.CompilerParams(
            dimension_semantics=("arbitrary", "arbitrary", "arbitrary"), vmem_limit_bytes=VMEM_LIMIT),
        name="diffattn",
    )(lamp, sub_g, dq, dk, dvt)


def _out_mlp_kernel(x_ref, om_ref, os_ref, od_ref, wo_ref, wup_ref, wdn_ref,
                    gpm_ref, gpre_ref, gpost_ref, out_ref, *, FC):
    mix = (_dot(om_ref[...], wo_ref[0:M_WIDTH, :])
           + _dot(os_ref[...], wo_ref[M_WIDTH:M_WIDTH + S_WIDTH, :])
           + _dot(od_ref[...], wo_ref[M_WIDTH + S_WIDTH:, :]))
    x1 = x_ref[...] + _rms(mix, gpm_ref[...])
    h2 = _rms(x1, gpre_ref[...]).astype(BF16)
    acc = jnp.zeros(x1.shape, F32)
    for c in range(D_FF // FC):
        u = jnp.maximum(_dot(h2, wup_ref[:, c * FC:(c + 1) * FC]), 0.0)
        acc = acc + _dot((u * u).astype(BF16), wdn_ref[c * FC:(c + 1) * FC, :])
    out_ref[...] = x1 + _rms(acc, gpost_ref[...])


def _out_mlp(x2d, om, os_, od, w_out, w_up, w_down, g_post_mix, g_pre_mlp, g_post_mlp, tm=512, FC=1024):
    N = x2d.shape[0]
    row = lambda width: pl.BlockSpec((tm, width), lambda i: (i, 0))
    const = lambda shape: pl.BlockSpec(shape, lambda i: (0, 0), pipeline_mode=pl.Buffered(1))
    return pl.pallas_call(
        functools.partial(_out_mlp_kernel, FC=FC),
        grid=(N // tm,),
        in_specs=[row(D_MODEL), row(M_WIDTH), row(S_WIDTH), row(D_WIDTH),
                  const((D_MODEL, D_MODEL)), const((D_MODEL, D_FF)), const((D_FF, D_MODEL)),
                  const((1, D_MODEL)), const((1, D_MODEL)), const((1, D_MODEL))],
        out_specs=row(D_MODEL),
        out_shape=jax.ShapeDtypeStruct((N, D_MODEL), F32),
        compiler_params=pltpu.CompilerParams(
            dimension_semantics=("arbitrary",), vmem_limit_bytes=VMEM_LIMIT),
        name="out_mlp",
    )(x2d, om, os_, od, w_out, w_up, w_down, g_post_mix, g_pre_mlp, g_post_mlp)


def _rope_tables(S):
    pos = jnp.arange(S, dtype=F32)[:, None]
    lane = np.arange(LANES)

    def table(head_dim):
        half = head_dim // 2
        inv = ROPE_THETA ** (-jnp.arange(half, dtype=F32) * 2.0 / head_dim)
        ang = pos * inv[None, :]
        idx = (lane % head_dim) % half
        sign = np.where((lane % head_dim) < half, -1.0, 1.0).astype(np.float32)
        return jnp.cos(ang)[:, idx], jnp.sin(ang)[:, idx] * sign

    c64, s64 = table(HEAD_DIM)
    c32, s32 = table(D_QK)
    return c64, s64, c32, s32


def _pack_w_in(w_in):
    sizes = (M_WIDTH, M_WIDTH, M_WIDTH, M_WIDTH, M_HEADS, M_HEADS,
             S_WIDTH, S_KV_WIDTH, S_KV_WIDTH, D_QK_WIDTH, D_QK_WIDTH, D_WIDTH)
    offs = np.concatenate([[0], np.cumsum(sizes)])
    part = [w_in[:, offs[i]:offs[i + 1]] for i in range(len(sizes))]
    mq, mk, mv, mo, mi, mf, sq, sk, sv, dq, dk, dv = part
    zeros = lambda n: jnp.zeros((D_MODEL, n), w_in.dtype)
    gates = jnp.concatenate([mi, mf, zeros(LANES - 2 * M_HEADS)], axis=1)
    dv_blocks = []
    for h in range(D_HEADS):
        dv_blocks += [dv[:, h * D_V:(h + 1) * D_V], zeros(LANES - D_V)]
    packed = jnp.concatenate([mq, mk, mv, mo, gates, sq, sk, sv, dq, dk], axis=1)
    wg_t = jnp.concatenate([mi, mf], axis=1).T
    wdv_t = jnp.concatenate(dv_blocks, axis=1).T
    return packed.astype(BF16), wg_t.astype(BF16), wdv_t.astype(BF16)


def kernel(x, w_in, conv_w, conv_b, i_bias, f_bias, m_norm_g, sinks, lam_q1, lam_k1, lam_q2, lam_k2,
           sub_g, w_out, w_up, w_down, g_pre_mix, g_post_mix, g_pre_mlp, g_post_mlp):
    B, S, D = x.shape
    depth = w_in.shape[0]
    tables = _rope_tables(S)
    pad_lanes = lambda v: jnp.pad(v, (0, LANES - v.shape[0]))[None, :]
    for l in range(depth):
        w_packed, wg_t, wdv_t = _pack_w_in(w_in[l])
        (qk, mv, mo, gate_c, gate_r, sq, sk, sv, dq, dk, dvt) = _in_proj(
            x, g_pre_mix[l][None, :], w_packed, wg_t, wdv_t, tables)

        gate_bias = jnp.concatenate([i_bias[l], f_bias[l]])
        out_m = _mlstm(qk, mv, mo, gate_c, gate_r, conv_w[l], conv_b[l][None, :],
                       pad_lanes(gate_bias), gate_bias[:, None], m_norm_g[l][None, :])
        out_s = _swa(sq, sk, sv, sinks[l])
        lamp = jnp.concatenate([pad_lanes(v) for v in (lam_q1[l], lam_k1[l], lam_q2[l], lam_k2[l])], axis=0)
        lam_init = 0.8 - 0.6 * math.exp(-0.3 * l)
        out_d = _diffattn(dq, dk, dvt, lamp, sub_g[l][None, :], lam_init)

        x = _out_mlp(x.reshape(B * S, D), out_m.reshape(B * S, -1), out_s.reshape(B * S, -1),
                     out_d.reshape(B * S, -1), w_out[l].astype(BF16), w_up[l].astype(BF16),
                     w_down[l].astype(BF16), g_post_mix[l][None, :], g_pre_mlp[l][None, :],
                     g_post_mlp[l][None, :]).reshape(B, S, D)
    return x
```

```python
import functools
import math

import jax
import jax.numpy as jnp
import numpy as np
from jax import lax
from jax.experimental import pallas as pl
from jax.experimental.pallas import tpu as pltpu

D_MODEL = 1024
M_HEADS = 4
M_DIM = 64
CONV_WIDTH = 4
CHUNK = 128
S_HEADS = 8
S_KV_HEADS = 2
HEAD_DIM = 64
WINDOW = 128
D_HEADS = 4
D_QK = 32
D_V = 64
ROPE_THETA = 10000.0
D_FF = 4 * D_MODEL
EPS = 1e-6

M_WIDTH = M_HEADS * M_DIM
S_WIDTH = S_HEADS * HEAD_DIM
S_KV_WIDTH = S_KV_HEADS * HEAD_DIM
D_QK_WIDTH = D_HEADS * 2 * D_QK
D_WIDTH = D_HEADS * D_V

LANES = 128
VMEM_LIMIT = 56 * 1024 * 1024

LOG2E = 1.4426950408889634

BF16 = jnp.bfloat16
F32 = jnp.float32

OFF_QK = 0
OFF_MV = 512
OFF_MO = 768
OFF_G = 1024
OFF_SK = 1152
OFF_DQ = 1280
OFF_DK = 1536
PACKED_WIDTH = 1792


def _nt_dot(a, b):
    return lax.dot_general(a, b, (((1,), (1,)), ((), ())), preferred_element_type=F32)


def _dot(a, b):
    return jnp.dot(a, b, preferred_element_type=F32)


def _rms(x, g):
    return x * lax.rsqrt(jnp.mean(x * x, axis=-1, keepdims=True) + EPS) * g


def _sigmoid(x):
    return 1.0 / (1.0 + jnp.exp(-x))


def _log_sigmoid(x):
    return jnp.minimum(x, 0.0) - jnp.log(1.0 + jnp.exp(-jnp.abs(x)))


def _split3(x):
    hi = x.astype(BF16)
    r1 = x - hi.astype(F32)
    mid = r1.astype(BF16)
    lo = (r1 - mid.astype(F32)).astype(BF16)
    return hi, mid, lo


def _rope_block(zb, cos, sin_signed, first_half, half):
    fwd = pltpu.roll(zb, LANES - half, 1)
    bwd = pltpu.roll(zb, half, 1)
    return zb * cos + jnp.where(first_half, fwd, bwd) * sin_signed


def _in_proj_kernel(x_ref, g_ref, w_ref, wgt_ref, wdvt_ref, wsqt_ref, wsvt_ref,
                    c64_ref, s64_ref, c32_ref, s32_ref, c64t_ref, s64t_ref,
                    qk_ref, mv_ref, mo_ref, gate_ref, gatet_ref,
                    sqt_ref, sk_ref, svt_ref, dq_ref, dk_ref, dvt_ref, h_scr):
    xf = x_ref[0]
    h_scr[...] = _rms(xf, g_ref[...]).astype(BF16)
    h = h_scr[...]

    def seg(off, width):
        return _dot(h, w_ref[:, off:off + width])

    qk_ref[0] = seg(OFF_QK, 512).astype(BF16)
    mv_ref[0] = seg(OFF_MV, 256).astype(BF16)
    mo_ref[0] = seg(OFF_MO, 256).astype(BF16)
    gate_ref[0] = seg(OFF_G, 128)
    gatet_ref[0] = _nt_dot(wgt_ref[...], h)

    lane = lax.broadcasted_iota(jnp.int32, (xf.shape[0], LANES), 1)
    first64 = (lane % 64) < 32
    first32 = (lane % 32) < 16
    c64, s64 = c64_ref[...], s64_ref[...]
    c32, s32 = c32_ref[...], s32_ref[...]

    sqt = _nt_dot(wsqt_ref[...], h)
    ct, st = c64t_ref[...], s64t_ref[...]
    half = HEAD_DIM // 2
    for hd in range(S_HEADS):
        x1 = sqt[hd * HEAD_DIM:hd * HEAD_DIM + half, :]
        x2 = sqt[hd * HEAD_DIM + half:(hd + 1) * HEAD_DIM, :]
        sqt_ref[0, hd * HEAD_DIM:hd * HEAD_DIM + half, :] = ((x1 * ct - x2 * st) * (HEAD_DIM ** -0.5)).astype(BF16)
        sqt_ref[0, hd * HEAD_DIM + half:(hd + 1) * HEAD_DIM, :] = ((x2 * ct + x1 * st) * (HEAD_DIM ** -0.5)).astype(BF16)
    sk_ref[0] = _rope_block(seg(OFF_SK, 128), c64, s64, first64, 32).astype(BF16)
    svt = _nt_dot(wsvt_ref[...], h)
    svt_feat = lax.broadcasted_iota(jnp.int32, svt.shape, 0) % LANES
    svt_ref[0] = jnp.where(svt_feat == HEAD_DIM, 1.0, svt).astype(BF16)

    dq = seg(OFF_DQ, 256)
    dk = seg(OFF_DK, 256)
    dq_scale = (D_QK ** -0.5) * LOG2E
    for c in range(2):
        sl = slice(c * LANES, (c + 1) * LANES)
        dq_ref[0, :, sl] = (_rope_block(dq[:, sl], c32, s32, first32, 16) * dq_scale).astype(BF16)
        dk_ref[0, :, sl] = _rope_block(dk[:, sl], c32, s32, first32, 16).astype(BF16)
    dvt = _nt_dot(wdvt_ref[...], h)
    feat = lax.broadcasted_iota(jnp.int32, dvt.shape, 0) % LANES
    dvt_ref[0] = jnp.where(feat == D_V, 1.0, dvt).astype(BF16)


def _in_proj(x, g, w_packed, wg_t, wdv_t, wsq_t, wsv_t, tables, tm=512):
    B, S, _ = x.shape
    c64, s64, c32, s32, c64t, s64t = tables
    tab_t = pl.BlockSpec((HEAD_DIM // 2, tm), lambda si, b: (0, si))
    row = lambda width: pl.BlockSpec((1, tm, width), lambda si, b: (b, si, 0))
    const = lambda shape: pl.BlockSpec(shape, lambda si, b: (0,) * len(shape))
    tab = pl.BlockSpec((tm, LANES), lambda si, b: (si, 0))
    out_shape = [
        jax.ShapeDtypeStruct((B, S, 512), BF16),
        jax.ShapeDtypeStruct((B, S, 256), BF16),
        jax.ShapeDtypeStruct((B, S, 256), BF16),
        jax.ShapeDtypeStruct((B, S, 128), F32),
        jax.ShapeDtypeStruct((B, 8, S), F32),
        jax.ShapeDtypeStruct((B, 512, S), BF16),
        jax.ShapeDtypeStruct((B, S, 128), BF16),
        jax.ShapeDtypeStruct((B, 256, S), BF16),
        jax.ShapeDtypeStruct((B, S, 256), BF16),
        jax.ShapeDtypeStruct((B, S, 256), BF16),
        jax.ShapeDtypeStruct((B, 512, S), BF16),
    ]
    out_specs = [row(512), row(256), row(256), row(128),
                 pl.BlockSpec((1, 8, tm), lambda si, b: (b, 0, si)),
                 pl.BlockSpec((1, 512, tm), lambda si, b: (b, 0, si)), row(128),
                 pl.BlockSpec((1, 256, tm), lambda si, b: (b, 0, si)), row(256), row(256),
                 pl.BlockSpec((1, 512, tm), lambda si, b: (b, 0, si))]
    return pl.pallas_call(
        _in_proj_kernel,
        grid=(S // tm, B),
        in_specs=[row(D_MODEL), const((1, D_MODEL)), const((D_MODEL, PACKED_WIDTH)),
                  const((8, D_MODEL)), const((512, D_MODEL)), const((512, D_MODEL)), const((256, D_MODEL)),
                  tab, tab, tab, tab, tab_t, tab_t],
        out_specs=out_specs,
        out_shape=out_shape,
        scratch_shapes=[pltpu.VMEM((tm, D_MODEL), BF16)],
        compiler_params=pltpu.CompilerParams(
            dimension_semantics=("arbitrary", "arbitrary"), vmem_limit_bytes=VMEM_LIMIT),
        name="in_proj",
    )(x, g, w_packed, wg_t, wdv_t, wsq_t, wsv_t, c64, s64, c32, s32, c64t, s64t)


def _mlstm_kernel(qk_ref, v_ref, o_ref, gc_ref, gr_ref, cw_ref, cb_ref, bc_ref, br_ref, ng_ref,
                  out_ref, xbuf, ct_scr, m_scr, *, T):
    L = CHUNK

    @pl.when(pl.program_id(1) == 0)
    def _():
        xbuf[0:8, :] = jnp.zeros((8, 2 * M_WIDTH), F32)
        ct_scr[...] = jnp.zeros_like(ct_scr)
        m_scr[...] = jnp.zeros_like(m_scr)

    xbuf[8:8 + T, :] = qk_ref[0].astype(F32)
    y = cb_ref[...] + sum(cw_ref[j:j + 1, :] * xbuf[5 + j:5 + j + T, :] for j in range(CONV_WIDTH))
    xbuf[0:8, :] = xbuf[T:T + 8, :]
    act = y * _sigmoid(y)
    q_all = act[:, :M_WIDTH].astype(BF16)
    k_f32 = act[:, M_WIDTH:] * (M_DIM ** -0.5)
    k_all = k_f32.astype(BF16)
    kt_all = k_f32.T.astype(BF16)

    v_f32 = v_ref[0].astype(F32)
    gc = gc_ref[0] + bc_ref[...]
    gr = gr_ref[0] + br_ref[...]
    logf_c = _log_sigmoid(gc)
    logf_r = _log_sigmoid(gr)

    ri = lax.broadcasted_iota(jnp.int32, (L, L), 0)
    ci = lax.broadcasted_iota(jnp.int32, (L, L), 1)
    causal = ci <= ri
    tri = jnp.where(causal, 1.0, 0.0).astype(BF16)
    tri_t = jnp.where(ri <= ci, 1.0, 0.0).astype(BF16)
    lane = lax.broadcasted_iota(jnp.int32, (L, LANES), 1)

    for c in range(T // L):
        r0 = c * L
        cum_c = sum(_dot(tri, p) for p in _split3(logf_c[r0:r0 + L, :]))
        cum_r = sum(_dot(p, tri_t) for p in _split3(logf_r[:, r0:r0 + L]))
        ig_c = gc[r0:r0 + L, :]
        ig_r = gr[:, r0:r0 + L]
        heads = []
        for h in range(M_HEADS):
            b_col = cum_c[:, 4 + h:5 + h]
            b_row = cum_r[4 + h:5 + h, :]
            i_row = ig_r[h:h + 1, :]
            i_col = ig_c[:, h:h + 1]
            m_prev = m_scr[h, 0:1, 0:1]

            q_h = q_all[r0:r0 + L, h * M_DIM:(h + 1) * M_DIM]
            k_h = k_all[r0:r0 + L, h * M_DIM:(h + 1) * M_DIM]
            kt_h = kt_all[h * M_DIM:(h + 1) * M_DIM, r0:r0 + L]
            vblk = v_f32[r0:r0 + L, (h // 2) * LANES:(h // 2 + 1) * LANES]
            if h % 2:
                vblk = pltpu.roll(vblk, M_DIM, 1)
            v_ext = jnp.where(lane < M_DIM, vblk, jnp.where(lane == M_DIM, 1.0, 0.0))

            dmat = jnp.where(causal, b_col - b_row + i_row, -jnp.inf)
            m_t = jnp.maximum(b_col + m_prev, jnp.max(dmat, axis=1, keepdims=True))
            w_intra = jnp.exp(dmat - m_t)
            sc = (_nt_dot(q_h, k_h) * w_intra).astype(BF16)
            a_inter = jnp.exp(b_col + m_prev - m_t)
            ct = ct_scr[h]
            num = a_inter * _dot(q_h, ct.astype(BF16)) + _dot(sc, v_ext.astype(BF16))
            den = num[:, M_DIM:M_DIM + 1]
            hh = num[:, :M_DIM] / jnp.maximum(jnp.abs(den), jnp.exp(-m_t))

            b_last = b_col[L - 1:L, :]
            g_col = b_last - b_col + i_col
            m_new = jnp.maximum(b_last + m_prev, jnp.max(g_col, axis=0, keepdims=True))
            w_state = jnp.exp(g_col - m_new)
            decay = jnp.exp(b_last + m_prev - m_new)
            ct_scr[h] = decay * ct + _dot(kt_h, (w_state * v_ext).astype(BF16))
            m_scr[h] = jnp.broadcast_to(m_new, (8, LANES))

            heads.append(_rms(hh, ng_ref[:, h * M_DIM:(h + 1) * M_DIM]))
        hcat = jnp.concatenate(heads, axis=-1)
        gate = _sigmoid(o_ref[0, r0:r0 + L, :].astype(F32))
        out_ref[0, r0:r0 + L, :] = (gate * hcat).astype(BF16)


def _mlstm(qk, mv, mo, gate_c, gate_r, conv_w, conv_b, bias_c, bias_r, norm_g, T=256):
    B, S, _ = qk.shape
    row = lambda width: pl.BlockSpec((1, T, width), lambda b, t: (b, t, 0))
    const = lambda shape: pl.BlockSpec(shape, lambda b, t: (0,) * len(shape))
    return pl.pallas_call(
        functools.partial(_mlstm_kernel, T=T),
        grid=(B, S // T),
        in_specs=[row(512), row(256), row(256), row(128),
                  pl.BlockSpec((1, 8, T), lambda b, t: (b, 0, t)),
                  const((CONV_WIDTH, 2 * M_WIDTH)), const((1, 2 * M_WIDTH)),
                  const((1, LANES)), const((8, 1)), const((1, M_WIDTH))],
        out_specs=row(M_WIDTH),
        out_shape=jax.ShapeDtypeStruct((B, S, M_WIDTH), BF16),
        scratch_shapes=[pltpu.VMEM((T + 8, 2 * M_WIDTH), F32),
                        pltpu.VMEM((M_HEADS, M_DIM, LANES), F32),
                        pltpu.VMEM((M_HEADS, 8, LANES), F32)],
        compiler_params=pltpu.CompilerParams(
            dimension_semantics=("arbitrary", "arbitrary"), vmem_limit_bytes=VMEM_LIMIT),
        name="mlstm",
    )(qk, mv, mo, gate_c, gate_r, conv_w, conv_b, bias_c, bias_r, norm_g)


def _swa_kernel(sink_ref, qt_ref, kc_ref, kp_ref, vtc_ref, vtp_ref, out_ref, s_scr, *, TQ):
    W = WINDOW
    G = S_HEADS // S_KV_HEADS
    tile = pl.program_id(1)
    c = lax.broadcasted_iota(jnp.int32, (2 * W, G * W), 0)
    r = lax.broadcasted_iota(jnp.int32, (2 * W, G * W), 1) % W
    band = (c > r) & (c <= r + W)
    units = [(j, hk) for j in range(TQ // W) for hk in range(S_KV_HEADS)]
    n_slots = s_scr.shape[0]

    def scores(u):
        j, hk = units[u]
        if j == 0:
            kband = jnp.concatenate([kp_ref[0], kc_ref[0, 0:W, :]], axis=0)
        else:
            kband = kc_ref[0, (j - 1) * W:(j + 1) * W, :]
        qs = jnp.concatenate(
            [qt_ref[0, (hk * G + g) * HEAD_DIM:(hk * G + g + 1) * HEAD_DIM, j * W:(j + 1) * W] for g in range(G)],
            axis=1)
        zeros = jnp.zeros_like(qs)
        qpad = jnp.concatenate([qs, zeros] if hk == 0 else [zeros, qs], axis=0)
        s_scr[u % n_slots] = _dot(kband, qpad)

    def softmax_pv(u):
        j, hk = units[u]
        rows = slice(hk * LANES, (hk + 1) * LANES)
        if j == 0:
            vt = jnp.concatenate([vtp_ref[0, rows, :], vtc_ref[0, rows, 0:W]], axis=1)
            valid = band & ((tile > 0) | (c >= W))
        else:
            vt = vtc_ref[0, rows, (j - 1) * W:(j + 1) * W]
            valid = band
        sink = jnp.concatenate([jnp.full((1, W), sink_ref[hk * G + g], F32) for g in range(G)], axis=1)
        s = jnp.where(valid, s_scr[u % n_slots], -jnp.inf)
        mx = jnp.maximum(jnp.max(s, axis=0, keepdims=True), sink)
        e = jnp.exp(s - mx).astype(BF16)
        o = _dot(vt, e)
        denom = o[HEAD_DIM:HEAD_DIM + 1, :] + jnp.exp(sink - mx)
        o = o[:HEAD_DIM, :] * (1.0 / denom)
        return [o[:, g * W:(g + 1) * W] for g in range(G)]

    scores(0)
    scores(1)
    heads = []
    for u in range(len(units)):
        if u + 2 < len(units):
            scores(u + 2)
        heads.extend(softmax_pv(u))
        if len(heads) == S_HEADS:
            j = units[u][0]
            out_ref[0, j * W:(j + 1) * W, :] = jnp.concatenate(heads, axis=0).T.astype(BF16)
            heads = []


def _swa(sqt, sk, svt, sinks, TQ=512):
    B, _, S = sqt.shape
    per = TQ // WINDOW
    prev_blk = lambda t: jnp.maximum(t * per - 1, 0)
    return pl.pallas_call(
        functools.partial(_swa_kernel, TQ=TQ),
        grid=(B, S // TQ),
        in_specs=[pl.BlockSpec(memory_space=pltpu.SMEM),
                  pl.BlockSpec((1, S_WIDTH, TQ), lambda b, t: (b, 0, t)),
                  pl.BlockSpec((1, TQ, S_KV_WIDTH), lambda b, t: (b, t, 0)),
                  pl.BlockSpec((1, WINDOW, S_KV_WIDTH), lambda b, t: (b, prev_blk(t), 0)),
                  pl.BlockSpec((1, S_KV_HEADS * LANES, TQ), lambda b, t: (b, 0, t)),
                  pl.BlockSpec((1, S_KV_HEADS * LANES, WINDOW), lambda b, t: (b, 0, prev_blk(t)))],
        out_specs=pl.BlockSpec((1, TQ, S_WIDTH), lambda b, t: (b, t, 0)),
        out_shape=jax.ShapeDtypeStruct((B, S, S_WIDTH), BF16),
        scratch_shapes=[pltpu.VMEM((3, 2 * WINDOW, (S_HEADS // S_KV_HEADS) * WINDOW), F32)],
        compiler_params=pltpu.CompilerParams(
            dimension_semantics=("arbitrary", "arbitrary"), vmem_limit_bytes=VMEM_LIMIT),
        name="swa",
    )(sinks, sqt, sk, sk, svt, svt)


def _diff_kernel(lamp_ref, subg_ref, q_ref, k_ref, vt_ref, out_ref, s_scr, m_scr, acc_scr, *, TQ, TK, lam_init):
    qi = pl.program_id(2)
    q = q_ref[0]
    lane = lax.broadcasted_iota(jnp.int32, (TQ, LANES), 1)
    zero = jnp.zeros_like(q)
    qg = [jnp.where((lane >= g * D_QK) & (lane < (g + 1) * D_QK), q, zero) for g in range(4)]

    m_scr[...] = jnp.full_like(m_scr, -jnp.inf)
    acc_scr[...] = jnp.zeros_like(acc_scr)

    key_in = lax.broadcasted_iota(jnp.int32, (TK, TQ), 0)
    q_pos = qi * TQ + lax.broadcasted_iota(jnp.int32, (TK, TQ), 1)

    def scores(j, g):
        start = pl.multiple_of(j * TK, TK)
        s_scr[g] = _nt_dot(k_ref[0, pl.ds(start, TK), :], qg[g])

    def softmax_pv(j, g, masked):
        start = pl.multiple_of(j * TK, TK)
        s = s_scr[g]
        if masked:
            s = jnp.where((start + key_in) <= q_pos, s, -jnp.inf)
        m_old = m_scr[g]
        m_new = jnp.maximum(m_old, jnp.max(s, axis=0, keepdims=True))
        p = jnp.exp2(s - m_new[0:1, :]).astype(BF16)
        alpha = jnp.exp2(m_old[0:1, :] - m_new[0:1, :])
        vt = vt_ref[0, (g // 2) * LANES:(g // 2 + 1) * LANES, pl.ds(start, TK)]
        acc_scr[g] = alpha * acc_scr[g] + _dot(vt, p)
        m_scr[g] = m_new

    def pipelined(j, masked, prefetch_next):
        scores(j, 2)
        softmax_pv(j, 0, masked)
        scores(j, 3)
        softmax_pv(j, 1, masked)
        if prefetch_next:
            scores(j + 1, 0)
        softmax_pv(j, 2, masked)
        if prefetch_next:
            scores(j + 1, 1)
        softmax_pv(j, 3, masked)

    n_full = (qi * TQ) // TK
    scores(0, 0)
    scores(0, 1)

    def full_body(j, carry):
        pipelined(j, False, True)
        return carry

    lax.fori_loop(0, n_full, full_body, 0)
    pipelined(n_full, True, False)

    lamp = lamp_ref[...]
    lam = (jnp.exp(jnp.sum(lamp[0:1] * lamp[1:2], axis=-1, keepdims=True))
           - jnp.exp(jnp.sum(lamp[2:3] * lamp[3:4], axis=-1, keepdims=True)) + lam_init)
    outs = []
    for hd in range(2):
        a1 = acc_scr[2 * hd].T
        a2 = acc_scr[2 * hd + 1].T
        o = a1[:, :D_V] / a1[:, D_V:D_V + 1] - lam * (a2[:, :D_V] / a2[:, D_V:D_V + 1])
        outs.append(_rms(o, subg_ref[...]) * (1.0 - lam_init))
    out_ref[0] = jnp.concatenate(outs, axis=-1).astype(BF16)


def _diffattn(dq, dk, dvt, lamp, sub_g, lam_init, TQ=512, TK=512):
    B, S, _ = dq.shape
    return pl.pallas_call(
        functools.partial(_diff_kernel, TQ=TQ, TK=TK, lam_init=lam_init),
        grid=(B, D_HEADS // 2, S // TQ),
        in_specs=[pl.BlockSpec((4, LANES), lambda b, hp, i: (0, 0)),
                  pl.BlockSpec((1, D_V), lambda b, hp, i: (0, 0)),
                  pl.BlockSpec((1, TQ, LANES), lambda b, hp, i: (b, i, hp)),
                  pl.BlockSpec((1, S, LANES), lambda b, hp, i: (b, 0, hp)),
                  pl.BlockSpec((1, 2 * LANES, S), lambda b, hp, i: (b, hp, 0))],
        out_specs=pl.BlockSpec((1, TQ, LANES), lambda b, hp, i: (b, i, hp)),
        out_shape=jax.ShapeDtypeStruct((B, S, D_WIDTH), BF16),
        scratch_shapes=[pltpu.VMEM((4, TK, TQ), F32), pltpu.VMEM((4, 8, TQ), F32),
                        pltpu.VMEM((4, LANES, TQ), F32)],
        compiler_params=pltpu.CompilerParams(
            dimension_semantics=("arbitrary", "arbitrary", "arbitrary"), vmem_limit_bytes=VMEM_LIMIT),
        name="diffattn",
    )(lamp, sub_g, dq, dk, dvt)


def _out_mlp_kernel(x_ref, om_ref, os_ref, od_ref, wo_ref, wup_ref, wdn_ref,
                    gpm_ref, gpre_ref, gpost_ref, out_ref, *, FC):
    mix = (_dot(om_ref[...], wo_ref[0:M_WIDTH, :])
           + _dot(os_ref[...], wo_ref[M_WIDTH:M_WIDTH + S_WIDTH, :])
           + _dot(od_ref[...], wo_ref[M_WIDTH + S_WIDTH:, :]))
    x1 = x_ref[...] + _rms(mix, gpm_ref[...])
    h2 = _rms(x1, gpre_ref[...]).astype(BF16)
    acc = jnp.zeros(x1.shape, F32)
    for c in range(D_FF // FC):
        u = jnp.maximum(_dot(h2, wup_ref[:, c * FC:(c + 1) * FC]), 0.0)
        acc = acc + _dot((u * u).astype(BF16), wdn_ref[c * FC:(c + 1) * FC, :])
    out_ref[...] = x1 + _rms(acc, gpost_ref[...])


def _out_mlp(x2d, om, os_, od, w_out, w_up, w_down, g_post_mix, g_pre_mlp, g_post_mlp, tm=512, FC=1024):
    N = x2d.shape[0]
    row = lambda width: pl.BlockSpec((tm, width), lambda i: (i, 0))
    const = lambda shape: pl.BlockSpec(shape, lambda i: (0, 0), pipeline_mode=pl.Buffered(1))
    return pl.pallas_call(
        functools.partial(_out_mlp_kernel, FC=FC),
        grid=(N // tm,),
        in_specs=[row(D_MODEL), row(M_WIDTH), row(S_WIDTH), row(D_WIDTH),
                  const((D_MODEL, D_MODEL)), const((D_MODEL, D_FF)), const((D_FF, D_MODEL)),
                  const((1, D_MODEL)), const((1, D_MODEL)), const((1, D_MODEL))],
        out_specs=row(D_MODEL),
        out_shape=jax.ShapeDtypeStruct((N, D_MODEL), F32),
        compiler_params=pltpu.CompilerParams(
            dimension_semantics=("arbitrary",), vmem_limit_bytes=VMEM_LIMIT),
        name="out_mlp",
    )(x2d, om, os_, od, w_out, w_up, w_down, g_post_mix, g_pre_mlp, g_post_mlp)


def _rope_tables(S):
    pos = jnp.arange(S, dtype=F32)[:, None]
    lane = np.arange(LANES)

    def table(head_dim):
        half = head_dim // 2
        inv = ROPE_THETA ** (-jnp.arange(half, dtype=F32) * 2.0 / head_dim)
        ang = pos * inv[None, :]
        idx = (lane % head_dim) % half
        sign = np.where((lane % head_dim) < half, -1.0, 1.0).astype(np.float32)
        return jnp.cos(ang)[:, idx], jnp.sin(ang)[:, idx] * sign

    c64, s64 = table(HEAD_DIM)
    c32, s32 = table(D_QK)
    inv = ROPE_THETA ** (-jnp.arange(HEAD_DIM // 2, dtype=F32) * 2.0 / HEAD_DIM)
    ang_t = inv[:, None] * pos.T
    return c64, s64, c32, s32, jnp.cos(ang_t), jnp.sin(ang_t)


def _pack_w_in(w_in):
    sizes = (M_WIDTH, M_WIDTH, M_WIDTH, M_WIDTH, M_HEADS, M_HEADS,
             S_WIDTH, S_KV_WIDTH, S_KV_WIDTH, D_QK_WIDTH, D_QK_WIDTH, D_WIDTH)
    offs = np.concatenate([[0], np.cumsum(sizes)])
    part = [w_in[:, offs[i]:offs[i + 1]] for i in range(len(sizes))]
    mq, mk, mv, mo, mi, mf, sq, sk, sv, dq, dk, dv = part
    zeros = lambda n: jnp.zeros((D_MODEL, n), w_in.dtype)
    gates = jnp.concatenate([mi, mf, zeros(LANES - 2 * M_HEADS)], axis=1)
    dv_blocks = []
    for h in range(D_HEADS):
        dv_blocks += [dv[:, h * D_V:(h + 1) * D_V], zeros(LANES - D_V)]
    sv_blocks = []
    for h in range(S_KV_HEADS):
        sv_blocks += [sv[:, h * HEAD_DIM:(h + 1) * HEAD_DIM], zeros(LANES - HEAD_DIM)]
    packed = jnp.concatenate([mq, mk, mv, mo, gates, sk, dq, dk], axis=1)
    wg_t = jnp.concatenate([mi, mf], axis=1).T
    wdv_t = jnp.concatenate(dv_blocks, axis=1).T
    wsv_t = jnp.concatenate(sv_blocks, axis=1).T
    return tuple(w.astype(BF16) for w in (packed, wg_t, wdv_t, sq.T, wsv_t))


def kernel(x, w_in, conv_w, conv_b, i_bias, f_bias, m_norm_g, sinks, lam_q1, lam_k1, lam_q2, lam_k2,
           sub_g, w_out, w_up, w_down, g_pre_mix, g_post_mix, g_pre_mlp, g_post_mlp):
    B, S, D = x.shape
    depth = w_in.shape[0]
    tables = _rope_tables(S)
    pad_lanes = lambda v: jnp.pad(v, (0, LANES - v.shape[0]))[None, :]
    for l in range(depth):
        (qk, mv, mo, gate_c, gate_r, sqt, sk, svt, dq, dk, dvt) = _in_proj(
            x, g_pre_mix[l][None, :], *_pack_w_in(w_in[l]), tables)

        gate_bias = jnp.concatenate([i_bias[l], f_bias[l]])
        out_m = _mlstm(qk, mv, mo, gate_c, gate_r, conv_w[l], conv_b[l][None, :],
                       pad_lanes(gate_bias), gate_bias[:, None], m_norm_g[l][None, :])
        out_s = _swa(sqt, sk, svt, sinks[l])
        lamp = jnp.concatenate([pad_lanes(v) for v in (lam_q1[l], lam_k1[l], lam_q2[l], lam_k2[l])], axis=0)
        lam_init = 0.8 - 0.6 * math.exp(-0.3 * l)
        out_d = _diffattn(dq, dk, dvt, lamp, sub_g[l][None, :], lam_init)

        x = _out_mlp(x.reshape(B * S, D), out_m.reshape(B * S, -1), out_s.reshape(B * S, -1),
                     out_d.reshape(B * S, -1), w_out[l].astype(BF16), w_up[l].astype(BF16),
                     w_down[l].astype(BF16), g_post_mix[l][None, :], g_pre_mlp[l][None, :],
                     g_post_mlp[l][None, :]).reshape(B, S, D)
    return x
```

```python
import functools
import math

import jax
import jax.numpy as jnp
import numpy as np
from jax import lax
from jax.experimental import pallas as pl
from jax.experimental.pallas import tpu as pltpu

D_MODEL = 1024
M_HEADS = 4
M_DIM = 64
CONV_WIDTH = 4
CHUNK = 128
S_HEADS = 8
S_KV_HEADS = 2
HEAD_DIM = 64
WINDOW = 128
D_HEADS = 4
D_QK = 32
D_V = 64
ROPE_THETA = 10000.0
D_FF = 4 * D_MODEL
EPS = 1e-6

M_WIDTH = M_HEADS * M_DIM
S_WIDTH = S_HEADS * HEAD_DIM
S_KV_WIDTH = S_KV_HEADS * HEAD_DIM
D_QK_WIDTH = D_HEADS * 2 * D_QK
D_WIDTH = D_HEADS * D_V

LANES = 128
VMEM_LIMIT = 56 * 1024 * 1024

LOG2E = 1.4426950408889634

BF16 = jnp.bfloat16
F32 = jnp.float32

OFF_QK = 0
OFF_MV = 512
OFF_MO = 768
OFF_G = 1024
OFF_SK = 1152
OFF_DQ = 1280
OFF_DK = 1536
PACKED_WIDTH = 1792


def _nt_dot(a, b):
    return lax.dot_general(a, b, (((1,), (1,)), ((), ())), preferred_element_type=F32)


def _dot(a, b):
    return jnp.dot(a, b, preferred_element_type=F32)


def _rms(x, g):
    return x * lax.rsqrt(jnp.mean(x * x, axis=-1, keepdims=True) + EPS) * g


def _sigmoid(x):
    return 1.0 / (1.0 + jnp.exp(-x))


def _log_sigmoid(x):
    return jnp.minimum(x, 0.0) - jnp.log(1.0 + jnp.exp(-jnp.abs(x)))


def _split3(x):
    hi = x.astype(BF16)
    r1 = x - hi.astype(F32)
    mid = r1.astype(BF16)
    lo = (r1 - mid.astype(F32)).astype(BF16)
    return hi, mid, lo


def _rope_block(zb, cos, sin_signed, first_half, half):
    fwd = pltpu.roll(zb, LANES - half, 1)
    bwd = pltpu.roll(zb, half, 1)
    return zb * cos + jnp.where(first_half, fwd, bwd) * sin_signed


def _in_proj_kernel(x_ref, g_ref, w_ref, wgt_ref, wdvt_ref, wsqt_ref, wsvt_ref,
                    c64_ref, s64_ref, c32_ref, s32_ref, c64t_ref, s64t_ref,
                    qk_ref, mv_ref, mo_ref, gate_ref, gatet_ref,
                    sqt_ref, sk_ref, svt_ref, dq_ref, dk_ref, dvt_ref, h_scr):
    xf = x_ref[0]
    h_scr[...] = _rms(xf, g_ref[...]).astype(BF16)
    h = h_scr[...]

    def seg(off, width):
        return _dot(h, w_ref[:, off:off + width])

    qk_ref[0] = seg(OFF_QK, 512).astype(BF16)
    mv_ref[0] = seg(OFF_MV, 256).astype(BF16)
    mo_ref[0] = seg(OFF_MO, 256).astype(BF16)
    gate_ref[0] = seg(OFF_G, 128)
    gatet_ref[0] = _nt_dot(wgt_ref[...], h)

    lane = lax.broadcasted_iota(jnp.int32, (xf.shape[0], LANES), 1)
    first64 = (lane % 64) < 32
    first32 = (lane % 32) < 16
    c64, s64 = c64_ref[...], s64_ref[...]
    c32, s32 = c32_ref[...], s32_ref[...]

    sqt = _nt_dot(wsqt_ref[...], h)
    ct, st = c64t_ref[...], s64t_ref[...]
    half = HEAD_DIM // 2
    for hd in range(S_HEADS):
        x1 = sqt[hd * HEAD_DIM:hd * HEAD_DIM + half, :]
        x2 = sqt[hd * HEAD_DIM + half:(hd + 1) * HEAD_DIM, :]
        sqt_ref[0, hd * HEAD_DIM:hd * HEAD_DIM + half, :] = ((x1 * ct - x2 * st) * (HEAD_DIM ** -0.5)).astype(BF16)
        sqt_ref[0, hd * HEAD_DIM + half:(hd + 1) * HEAD_DIM, :] = ((x2 * ct + x1 * st) * (HEAD_DIM ** -0.5)).astype(BF16)
    sk_ref[0] = _rope_block(seg(OFF_SK, 128), c64, s64, first64, 32).astype(BF16)
    svt = _nt_dot(wsvt_ref[...], h)
    svt_feat = lax.broadcasted_iota(jnp.int32, svt.shape, 0) % LANES
    svt_ref[0] = jnp.where(svt_feat == HEAD_DIM, 1.0, svt).astype(BF16)

    dq = seg(OFF_DQ, 256)
    dk = seg(OFF_DK, 256)
    dq_scale = (D_QK ** -0.5) * LOG2E
    for c in range(2):
        sl = slice(c * LANES, (c + 1) * LANES)
        dq_ref[0, :, sl] = (_rope_block(dq[:, sl], c32, s32, first32, 16) * dq_scale).astype(BF16)
        dk_ref[0, :, sl] = _rope_block(dk[:, sl], c32, s32, first32, 16).astype(BF16)
    dvt = _nt_dot(wdvt_ref[...], h)
    feat = lax.broadcasted_iota(jnp.int32, dvt.shape, 0) % LANES
    dvt_ref[0] = jnp.where(feat == D_V, 1.0, dvt).astype(BF16)


def _in_proj(x, g, w_packed, wg_t, wdv_t, wsq_t, wsv_t, tables, tm=512):
    B, S, _ = x.shape
    c64, s64, c32, s32, c64t, s64t = tables
    tab_t = pl.BlockSpec((HEAD_DIM // 2, tm), lambda si, b: (0, si))
    row = lambda width: pl.BlockSpec((1, tm, width), lambda si, b: (b, si, 0))
    const = lambda shape: pl.BlockSpec(shape, lambda si, b: (0,) * len(shape))
    tab = pl.BlockSpec((tm, LANES), lambda si, b: (si, 0))
    out_shape = [
        jax.ShapeDtypeStruct((B, S, 512), BF16),
        jax.ShapeDtypeStruct((B, S, 256), BF16),
        jax.ShapeDtypeStruct((B, S, 256), BF16),
        jax.ShapeDtypeStruct((B, S, 128), F32),
        jax.ShapeDtypeStruct((B, 8, S), F32),
        jax.ShapeDtypeStruct((B, 512, S), BF16),
        jax.ShapeDtypeStruct((B, S, 128), BF16),
        jax.ShapeDtypeStruct((B, 256, S), BF16),
        jax.ShapeDtypeStruct((B, S, 256), BF16),
        jax.ShapeDtypeStruct((B, S, 256), BF16),
        jax.ShapeDtypeStruct((B, 512, S), BF16),
    ]
    out_specs = [row(512), row(256), row(256), row(128),
                 pl.BlockSpec((1, 8, tm), lambda si, b: (b, 0, si)),
                 pl.BlockSpec((1, 512, tm), lambda si, b: (b, 0, si)), row(128),
                 pl.BlockSpec((1, 256, tm), lambda si, b: (b, 0, si)), row(256), row(256),
                 pl.BlockSpec((1, 512, tm), lambda si, b: (b, 0, si))]
    return pl.pallas_call(
        _in_proj_kernel,
        grid=(S // tm, B),
        in_specs=[row(D_MODEL), const((1, D_MODEL)), const((D_MODEL, PACKED_WIDTH)),
                  const((8, D_MODEL)), const((512, D_MODEL)), const((512, D_MODEL)), const((256, D_MODEL)),
                  tab, tab, tab, tab, tab_t, tab_t],
        out_specs=out_specs,
        out_shape=out_shape,
        scratch_shapes=[pltpu.VMEM((tm, D_MODEL), BF16)],
        compiler_params=pltpu.CompilerParams(
            dimension_semantics=("arbitrary", "arbitrary"), vmem_limit_bytes=VMEM_LIMIT),
        name="in_proj",
    )(x, g, w_packed, wg_t, wdv_t, wsq_t, wsv_t, c64, s64, c32, s32, c64t, s64t)


def _mlstm_kernel(qk_ref, v_ref, o_ref, gc_ref, gr_ref, cw_ref, cb_ref, bc_ref, br_ref, ngt_ref,
                  out_ref, xbuf, c_scr, m_scr, *, T):
    L = CHUNK

    @pl.when(pl.program_id(1) == 0)
    def _():
        xbuf[0:8, :] = jnp.zeros((8, 2 * M_WIDTH), F32)
        c_scr[...] = jnp.zeros_like(c_scr)
        m_scr[...] = jnp.zeros_like(m_scr)

    xbuf[8:8 + T, :] = qk_ref[0].astype(F32)
    y = cb_ref[...] + sum(cw_ref[j:j + 1, :] * xbuf[5 + j:5 + j + T, :] for j in range(CONV_WIDTH))
    xbuf[0:8, :] = xbuf[T:T + 8, :]
    act = y * _sigmoid(y)
    q_all = act[:, :M_WIDTH].astype(BF16)
    k_all = (act[:, M_WIDTH:] * (M_DIM ** -0.5)).astype(BF16)
    vt_all = v_ref[0].astype(F32).T

    gc = gc_ref[0] + bc_ref[...]
    gr = gr_ref[0] + br_ref[...]
    logf_c = _log_sigmoid(gc)
    logf_r = _log_sigmoid(gr)

    si = lax.broadcasted_iota(jnp.int32, (L, L), 0)
    ti = lax.broadcasted_iota(jnp.int32, (L, L), 1)
    keep = si <= ti
    tri = jnp.where(ti <= si, 1.0, 0.0).astype(BF16)
    tri_t = jnp.where(keep, 1.0, 0.0).astype(BF16)
    ones_rows = jnp.where(lax.broadcasted_iota(jnp.int32, (LANES - M_DIM, L), 0) == 0, 1.0, 0.0)

    for c in range(T // L):
        r0 = c * L
        cum_c = sum(_dot(tri, p) for p in _split3(logf_c[r0:r0 + L, :]))
        cum_r = sum(_dot(p, tri_t) for p in _split3(logf_r[:, r0:r0 + L]))
        ig_c = gc[r0:r0 + L, :]
        ig_r = gr[:, r0:r0 + L]

        staged = []
        for h in range(M_HEADS):
            b_row = cum_r[4 + h:5 + h, :]
            i_row = ig_r[h:h + 1, :]
            m_prev = m_scr[h, 0:1, 0:1]
            q_h = q_all[r0:r0 + L, h * M_DIM:(h + 1) * M_DIM]
            k_h = k_all[r0:r0 + L, h * M_DIM:(h + 1) * M_DIM]
            vt_ext = jnp.concatenate([vt_all[h * M_DIM:(h + 1) * M_DIM, r0:r0 + L], ones_rows], axis=0)

            b_last = b_row[:, L - 1:L]
            g_row = b_last - b_row + i_row
            m_new = jnp.maximum(b_last + m_prev, jnp.max(g_row, axis=1, keepdims=True))
            w_state = jnp.exp(g_row - m_new)
            decay = jnp.exp(b_last + m_prev - m_new)

            s_t = _nt_dot(k_h, q_h)
            c_old = c_scr[h]
            inter = _nt_dot(c_old.astype(BF16), q_h)
            c_scr[h] = decay * c_old + _dot((vt_ext * w_state).astype(BF16), k_h)
            m_scr[h] = jnp.broadcast_to(m_new, (8, LANES))
            staged.append((s_t, inter, vt_ext, b_row, m_prev))

        heads = []
        for h in range(M_HEADS):
            s_t, inter, vt_ext, b_row, m_prev = staged[h]
            col = ig_c[:, h:h + 1] - cum_c[:, 4 + h:5 + h]
            dmat = jnp.where(keep, b_row + col, -jnp.inf)
            m_t = jnp.maximum(b_row + m_prev, jnp.max(dmat, axis=0, keepdims=True))
            sc = (s_t * jnp.exp(dmat - m_t)).astype(BF16)
            a_inter = jnp.exp(b_row + m_prev - m_t)
            num = a_inter * inter + _dot(vt_ext.astype(BF16), sc)
            den = num[M_DIM:M_DIM + 1, :]
            hh = num[:M_DIM, :] * (1.0 / jnp.maximum(jnp.abs(den), jnp.exp(-m_t)))
            inv = lax.rsqrt(jnp.mean(hh * hh, axis=0, keepdims=True) + EPS)
            heads.append(hh * inv * ngt_ref[h * M_DIM:(h + 1) * M_DIM, :])
        hcat = jnp.concatenate(heads, axis=0).T
        gate = _sigmoid(o_ref[0, r0:r0 + L, :].astype(F32))
        out_ref[0, r0:r0 + L, :] = (gate * hcat).astype(BF16)


def _mlstm(qk, mv, mo, gate_c, gate_r, conv_w, conv_b, bias_c, bias_r, norm_g, T=256):
    B, S, _ = qk.shape
    row = lambda width: pl.BlockSpec((1, T, width), lambda b, t: (b, t, 0))
    const = lambda shape: pl.BlockSpec(shape, lambda b, t: (0,) * len(shape))
    return pl.pallas_call(
        functools.partial(_mlstm_kernel, T=T),
        grid=(B, S // T),
        in_specs=[row(512), row(256), row(256), row(128),
                  pl.BlockSpec((1, 8, T), lambda b, t: (b, 0, t)),
                  const((CONV_WIDTH, 2 * M_WIDTH)), const((1, 2 * M_WIDTH)),
                  const((1, LANES)), const((8, 1)), const((M_WIDTH, LANES))],
        out_specs=row(M_WIDTH),
        out_shape=jax.ShapeDtypeStruct((B, S, M_WIDTH), BF16),
        scratch_shapes=[pltpu.VMEM((T + 8, 2 * M_WIDTH), F32),
                        pltpu.VMEM((M_HEADS, LANES, M_DIM), F32),
                        pltpu.VMEM((M_HEADS, 8, LANES), F32)],
        compiler_params=pltpu.CompilerParams(
            dimension_semantics=("arbitrary", "arbitrary"), vmem_limit_bytes=VMEM_LIMIT),
        name="mlstm",
    )(qk, mv, mo, gate_c, gate_r, conv_w, conv_b, bias_c, bias_r, norm_g)


def _swa_kernel(sink_ref, qt_ref, kc_ref, kp_ref, vtc_ref, vtp_ref, out_ref, s_scr, *, TQ):
    W = WINDOW
    G = S_HEADS // S_KV_HEADS
    tile = pl.program_id(1)
    c = lax.broadcasted_iota(jnp.int32, (2 * W, G * W), 0)
    r = lax.broadcasted_iota(jnp.int32, (2 * W, G * W), 1) % W
    band = (c > r) & (c <= r + W)
    units = [(j, hk) for j in range(TQ // W) for hk in range(S_KV_HEADS)]
    n_slots = s_scr.shape[0]

    def scores(u):
        j, hk = units[u]
        if j == 0:
            kband = jnp.concatenate([kp_ref[0], kc_ref[0, 0:W, :]], axis=0)
        else:
            kband = kc_ref[0, (j - 1) * W:(j + 1) * W, :]
        qs = jnp.concatenate(
            [qt_ref[0, (hk * G + g) * HEAD_DIM:(hk * G + g + 1) * HEAD_DIM, j * W:(j + 1) * W] for g in range(G)],
            axis=1)
        zeros = jnp.zeros_like(qs)
        qpad = jnp.concatenate([qs, zeros] if hk == 0 else [zeros, qs], axis=0)
        s_scr[u % n_slots] = _dot(kband, qpad)

    def softmax_pv(u):
        j, hk = units[u]
        rows = slice(hk * LANES, (hk + 1) * LANES)
        if j == 0:
            vt = jnp.concatenate([vtp_ref[0, rows, :], vtc_ref[0, rows, 0:W]], axis=1)
            valid = band & ((tile > 0) | (c >= W))
        else:
            vt = vtc_ref[0, rows, (j - 1) * W:(j + 1) * W]
            valid = band
        sink = jnp.concatenate([jnp.full((1, W), sink_ref[hk * G + g], F32) for g in range(G)], axis=1)
        s = jnp.where(valid, s_scr[u % n_slots], -jnp.inf)
        mx = jnp.maximum(jnp.max(s, axis=0, keepdims=True), sink)
        e = jnp.exp(s - mx).astype(BF16)
        o = _dot(vt, e)
        denom = o[HEAD_DIM:HEAD_DIM + 1, :] + jnp.exp(sink - mx)
        o = o[:HEAD_DIM, :] * (1.0 / denom)
        return [o[:, g * W:(g + 1) * W] for g in range(G)]

    scores(0)
    scores(1)
    heads = []
    for u in range(len(units)):
        if u + 2 < len(units):
            scores(u + 2)
        heads.extend(softmax_pv(u))
        if len(heads) == S_HEADS:
            j = units[u][0]
            out_ref[0, j * W:(j + 1) * W, :] = jnp.concatenate(heads, axis=0).T.astype(BF16)
            heads = []


def _swa(sqt, sk, svt, sinks, TQ=512):
    B, _, S = sqt.shape
    per = TQ // WINDOW
    prev_blk = lambda t: jnp.maximum(t * per - 1, 0)
    return pl.pallas_call(
        functools.partial(_swa_kernel, TQ=TQ),
        grid=(B, S // TQ),
        in_specs=[pl.BlockSpec(memory_space=pltpu.SMEM),
                  pl.BlockSpec((1, S_WIDTH, TQ), lambda b, t: (b, 0, t)),
                  pl.BlockSpec((1, TQ, S_KV_WIDTH), lambda b, t: (b, t, 0)),
                  pl.BlockSpec((1, WINDOW, S_KV_WIDTH), lambda b, t: (b, prev_blk(t), 0)),
                  pl.BlockSpec((1, S_KV_HEADS * LANES, TQ), lambda b, t: (b, 0, t)),
                  pl.BlockSpec((1, S_KV_HEADS * LANES, WINDOW), lambda b, t: (b, 0, prev_blk(t)))],
        out_specs=pl.BlockSpec((1, TQ, S_WIDTH), lambda b, t: (b, t, 0)),
        out_shape=jax.ShapeDtypeStruct((B, S, S_WIDTH), BF16),
        scratch_shapes=[pltpu.VMEM((3, 2 * WINDOW, (S_HEADS // S_KV_HEADS) * WINDOW), F32)],
        compiler_params=pltpu.CompilerParams(
            dimension_semantics=("arbitrary", "arbitrary"), vmem_limit_bytes=VMEM_LIMIT),
        name="swa",
    )(sinks, sqt, sk, sk, svt, svt)


def _diff_kernel(lamp_ref, subg_ref, q_ref, k_ref, vt_ref, out_ref, s_scr, m_scr, acc_scr, *, TQ, TK, lam_init):
    qi = pl.program_id(2)
    q = q_ref[0]
    lane = lax.broadcasted_iota(jnp.int32, (TQ, LANES), 1)
    zero = jnp.zeros_like(q)
    qg = [jnp.where((lane >= g * D_QK) & (lane < (g + 1) * D_QK), q, zero) for g in range(4)]

    m_scr[...] = jnp.full_like(m_scr, -jnp.inf)
    acc_scr[...] = jnp.zeros_like(acc_scr)

    assert TQ == TK
    below_diag = (lax.broadcasted_iota(jnp.int32, (TK, TQ), 0)
                  <= lax.broadcasted_iota(jnp.int32, (TK, TQ), 1))

    def scores(j, g):
        start = pl.multiple_of(j * TK, TK)
        s_scr[g] = _nt_dot(k_ref[0, pl.ds(start, TK), :], qg[g])

    def softmax_pv(j, g, masked):
        start = pl.multiple_of(j * TK, TK)
        s = s_scr[g]
        if masked:
            s = jnp.where(below_diag, s, -jnp.inf)
        m_old = m_scr[g]
        m_new = jnp.maximum(m_old, jnp.max(s, axis=0, keepdims=True))
        p = jnp.exp2(s - m_new[0:1, :]).astype(BF16)
        alpha = jnp.exp2(m_old[0:1, :] - m_new[0:1, :])
        vt = vt_ref[0, (g // 2) * LANES:(g // 2 + 1) * LANES, pl.ds(start, TK)]
        acc_scr[g] = alpha * acc_scr[g] + _dot(vt, p)
        m_scr[g] = m_new

    def pipelined(j, masked, prefetch_next):
        scores(j, 2)
        softmax_pv(j, 0, masked)
        scores(j, 3)
        softmax_pv(j, 1, masked)
        if prefetch_next:
            scores(j + 1, 0)
        softmax_pv(j, 2, masked)
        if prefetch_next:
            scores(j + 1, 1)
        softmax_pv(j, 3, masked)

    n_full = (qi * TQ) // TK
    scores(0, 0)
    scores(0, 1)

    def full_body(j, carry):
        pipelined(j, False, True)
        return carry

    lax.fori_loop(0, n_full, full_body, 0)
    pipelined(n_full, True, False)

    lamp = lamp_ref[...]
    lam = (jnp.exp(jnp.sum(lamp[0:1] * lamp[1:2], axis=-1, keepdims=True))
           - jnp.exp(jnp.sum(lamp[2:3] * lamp[3:4], axis=-1, keepdims=True)) + lam_init)
    outs = []
    g_t = jnp.concatenate([subg_ref[...]] * (TQ // LANES), axis=1) * (1.0 - lam_init)
    for hd in range(2):
        a1 = acc_scr[2 * hd]
        a2 = acc_scr[2 * hd + 1]
        o = a1[:D_V, :] * (1.0 / a1[D_V:D_V + 1, :]) - lam * (a2[:D_V, :] * (1.0 / a2[D_V:D_V + 1, :]))
        outs.append(o * lax.rsqrt(jnp.mean(o * o, axis=0, keepdims=True) + EPS) * g_t)
    out_ref[0] = jnp.concatenate(outs, axis=0).T.astype(BF16)


def _diffattn(dq, dk, dvt, lamp, sub_g, lam_init, TQ=512, TK=512):
    B, S, _ = dq.shape
    return pl.pallas_call(
        functools.partial(_diff_kernel, TQ=TQ, TK=TK, lam_init=lam_init),
        grid=(B, D_HEADS // 2, S // TQ),
        in_specs=[pl.BlockSpec((4, LANES), lambda b, hp, i: (0, 0)),
                  pl.BlockSpec((D_V, LANES), lambda b, hp, i: (0, 0)),
                  pl.BlockSpec((1, TQ, LANES), lambda b, hp, i: (b, i, hp)),
                  pl.BlockSpec((1, S, LANES), lambda b, hp, i: (b, 0, hp)),
                  pl.BlockSpec((1, 2 * LANES, S), lambda b, hp, i: (b, hp, 0))],
        out_specs=pl.BlockSpec((1, TQ, LANES), lambda b, hp, i: (b, i, hp)),
        out_shape=jax.ShapeDtypeStruct((B, S, D_WIDTH), BF16),
        scratch_shapes=[pltpu.VMEM((4, TK, TQ), F32), pltpu.VMEM((4, 8, TQ), F32),
                        pltpu.VMEM((4, LANES, TQ), F32)],
        compiler_params=pltpu.CompilerParams(
            dimension_semantics=("arbitrary", "arbitrary", "arbitrary"), vmem_limit_bytes=VMEM_LIMIT),
        name="diffattn",
    )(lamp, sub_g, dq, dk, dvt)


def _out_mlp_kernel(x_ref, om_ref, os_ref, od_ref, wo_ref, wup_ref, wdn_ref,
                    gpm_ref, gpre_ref, gpost_ref, out_ref, *, FC):
    mix = (_dot(om_ref[...], wo_ref[0:M_WIDTH, :])
           + _dot(os_ref[...], wo_ref[M_WIDTH:M_WIDTH + S_WIDTH, :])
           + _dot(od_ref[...], wo_ref[M_WIDTH + S_WIDTH:, :]))
    x1 = x_ref[...] + _rms(mix, gpm_ref[...])
    h2 = _rms(x1, gpre_ref[...]).astype(BF16)
    acc = jnp.zeros(x1.shape, F32)
    for c in range(D_FF // FC):
        u = jnp.maximum(_dot(h2, wup_ref[:, c * FC:(c + 1) * FC]), 0.0)
        acc = acc + _dot((u * u).astype(BF16), wdn_ref[c * FC:(c + 1) * FC, :])
    out_ref[...] = x1 + _rms(acc, gpost_ref[...])


def _out_mlp(x2d, om, os_, od, w_out, w_up, w_down, g_post_mix, g_pre_mlp, g_post_mlp, tm=512, FC=1024):
    N = x2d.shape[0]
    row = lambda width: pl.BlockSpec((tm, width), lambda i: (i, 0))
    const = lambda shape: pl.BlockSpec(shape, lambda i: (0, 0), pipeline_mode=pl.Buffered(1))
    return pl.pallas_call(
        functools.partial(_out_mlp_kernel, FC=FC),
        grid=(N // tm,),
        in_specs=[row(D_MODEL), row(M_WIDTH), row(S_WIDTH), row(D_WIDTH),
                  const((D_MODEL, D_MODEL)), const((D_MODEL, D_FF)), const((D_FF, D_MODEL)),
                  const((1, D_MODEL)), const((1, D_MODEL)), const((1, D_MODEL))],
        out_specs=row(D_MODEL),
        out_shape=jax.ShapeDtypeStruct((N, D_MODEL), F32),
        compiler_params=pltpu.CompilerParams(
            dimension_semantics=("arbitrary",), vmem_limit_bytes=VMEM_LIMIT),
        name="out_mlp",
    )(x2d, om, os_, od, w_out, w_up, w_down, g_post_mix, g_pre_mlp, g_post_mlp)


def _rope_tables(S):
    pos = jnp.arange(S, dtype=F32)[:, None]
    lane = np.arange(LANES)

    def table(head_dim):
        half = head_dim // 2
        inv = ROPE_THETA ** (-jnp.arange(half, dtype=F32) * 2.0 / head_dim)
        ang = pos * inv[None, :]
        idx = (lane % head_dim) % half
        sign = np.where((lane % head_dim) < half, -1.0, 1.0).astype(np.float32)
        return jnp.cos(ang)[:, idx], jnp.sin(ang)[:, idx] * sign

    c64, s64 = table(HEAD_DIM)
    c32, s32 = table(D_QK)
    inv = ROPE_THETA ** (-jnp.arange(HEAD_DIM // 2, dtype=F32) * 2.0 / HEAD_DIM)
    ang_t = inv[:, None] * pos.T
    return c64, s64, c32, s32, jnp.cos(ang_t), jnp.sin(ang_t)


def _pack_w_in(w_in):
    sizes = (M_WIDTH, M_WIDTH, M_WIDTH, M_WIDTH, M_HEADS, M_HEADS,
             S_WIDTH, S_KV_WIDTH, S_KV_WIDTH, D_QK_WIDTH, D_QK_WIDTH, D_WIDTH)
    offs = np.concatenate([[0], np.cumsum(sizes)])
    part = [w_in[:, offs[i]:offs[i + 1]] for i in range(len(sizes))]
    mq, mk, mv, mo, mi, mf, sq, sk, sv, dq, dk, dv = part
    zeros = lambda n: jnp.zeros((D_MODEL, n), w_in.dtype)
    gates = jnp.concatenate([mi, mf, zeros(LANES - 2 * M_HEADS)], axis=1)
    dv_blocks = []
    for h in range(D_HEADS):
        dv_blocks += [dv[:, h * D_V:(h + 1) * D_V], zeros(LANES - D_V)]
    sv_blocks = []
    for h in range(S_KV_HEADS):
        sv_blocks += [sv[:, h * HEAD_DIM:(h + 1) * HEAD_DIM], zeros(LANES - HEAD_DIM)]
    packed = jnp.concatenate([mq, mk, mv, mo, gates, sk, dq, dk], axis=1)
    wg_t = jnp.concatenate([mi, mf], axis=1).T
    wdv_t = jnp.concatenate(dv_blocks, axis=1).T
    wsv_t = jnp.concatenate(sv_blocks, axis=1).T
    return tuple(w.astype(BF16) for w in (packed, wg_t, wdv_t, sq.T, wsv_t))


def kernel(x, w_in, conv_w, conv_b, i_bias, f_bias, m_norm_g, sinks, lam_q1, lam_k1, lam_q2, lam_k2,
           sub_g, w_out, w_up, w_down, g_pre_mix, g_post_mix, g_pre_mlp, g_post_mlp):
    B, S, D = x.shape
    depth = w_in.shape[0]
    tables = _rope_tables(S)
    pad_lanes = lambda v: jnp.pad(v, (0, LANES - v.shape[0]))[None, :]
    lane_bcast = lambda v: jnp.broadcast_to(v[:, None], (v.shape[0], LANES))
    for l in range(depth):
        (qk, mv, mo, gate_c, gate_r, sqt, sk, svt, dq, dk, dvt) = _in_proj(
            x, g_pre_mix[l][None, :], *_pack_w_in(w_in[l]), tables)

        gate_bias = jnp.concatenate([i_bias[l], f_bias[l]])
        out_m = _mlstm(qk, mv, mo, gate_c, gate_r, conv_w[l], conv_b[l][None, :],
                       pad_lanes(gate_bias), gate_bias[:, None], lane_bcast(m_norm_g[l]))
        out_s = _swa(sqt, sk, svt, sinks[l])
        lamp = jnp.concatenate([pad_lanes(v) for v in (lam_q1[l], lam_k1[l], lam_q2[l], lam_k2[l])], axis=0)
        lam_init = 0.8 - 0.6 * math.exp(-0.3 * l)
        out_d = _diffattn(dq, dk, dvt, lamp, lane_bcast(sub_g[l]), lam_init)

        x = _out_mlp(x.reshape(B * S, D), out_m.reshape(B * S, -1), out_s.reshape(B * S, -1),
                     out_d.reshape(B * S, -1), w_out[l].astype(BF16), w_up[l].astype(BF16),
                     w_down[l].astype(BF16), g_post_mix[l][None, :], g_pre_mlp[l][None, :],
                     g_post_mlp[l][None, :]).reshape(B, S, D)
    return x
```

```python
import functools
import math

import jax
import jax.numpy as jnp
import numpy as np
from jax import lax
from jax.experimental import pallas as pl
from jax.experimental.pallas import tpu as pltpu

D_MODEL = 1024
M_HEADS = 4
M_DIM = 64
CONV_WIDTH = 4
CHUNK = 128
S_HEADS = 8
S_KV_HEADS = 2
HEAD_DIM = 64
WINDOW = 128
D_HEADS = 4
D_QK = 32
D_V = 64
ROPE_THETA = 10000.0
D_FF = 4 * D_MODEL
EPS = 1e-6

M_WIDTH = M_HEADS * M_DIM
S_WIDTH = S_HEADS * HEAD_DIM
S_KV_WIDTH = S_KV_HEADS * HEAD_DIM
D_QK_WIDTH = D_HEADS * 2 * D_QK
D_WIDTH = D_HEADS * D_V
D_VX = D_V + 16

LANES = 128
VMEM_LIMIT = 56 * 1024 * 1024

LOG2E = 1.4426950408889634

BF16 = jnp.bfloat16
F32 = jnp.float32

OFF_QK = 0
OFF_MV = 512
OFF_MO = 768
OFF_G = 1024
OFF_SK = 1152
OFF_DQ = 1280
OFF_DK = 1536
PACKED_WIDTH = 1792


def _nt_dot(a, b):
    return lax.dot_general(a, b, (((1,), (1,)), ((), ())), preferred_element_type=F32)


def _dot(a, b):
    return jnp.dot(a, b, preferred_element_type=F32)


def _rms(x, g):
    return x * lax.rsqrt(jnp.mean(x * x, axis=-1, keepdims=True) + EPS) * g


def _sigmoid(x):
    return 1.0 / (1.0 + jnp.exp(-x))


def _log_sigmoid(x):
    return jnp.minimum(x, 0.0) - jnp.log(1.0 + jnp.exp(-jnp.abs(x)))


def _split3(x):
    hi = x.astype(BF16)
    r1 = x - hi.astype(F32)
    mid = r1.astype(BF16)
    lo = (r1 - mid.astype(F32)).astype(BF16)
    return hi, mid, lo


def _rope_block(zb, cos, sin_signed, first_half, half):
    fwd = pltpu.roll(zb, LANES - half, 1)
    bwd = pltpu.roll(zb, half, 1)
    return zb * cos + jnp.where(first_half, fwd, bwd) * sin_signed


def _in_proj_kernel(x_ref, g_ref, w_ref, wgt_ref, wdvt_ref, wsqt_ref, wsvt_ref,
                    c64_ref, s64_ref, c32_ref, s32_ref, c64t_ref, s64t_ref,
                    qk_ref, mv_ref, mo_ref, gate_ref, gatet_ref,
                    sqt_ref, sk_ref, svt_ref, dq_ref, dk_ref, dvt_ref, h_scr):
    xf = x_ref[0]
    h_scr[...] = _rms(xf, g_ref[...]).astype(BF16)
    h = h_scr[...]

    def seg(off, width):
        return _dot(h, w_ref[:, off:off + width])

    qk_ref[0] = seg(OFF_QK, 512).astype(BF16)
    mv_ref[0] = seg(OFF_MV, 256).astype(BF16)
    mo_ref[0] = seg(OFF_MO, 256).astype(BF16)
    gate_ref[0] = seg(OFF_G, 128)
    gatet_ref[0] = _nt_dot(wgt_ref[...], h)

    lane = lax.broadcasted_iota(jnp.int32, (xf.shape[0], LANES), 1)
    first64 = (lane % 64) < 32
    first32 = (lane % 32) < 16
    c64, s64 = c64_ref[...], s64_ref[...]
    c32, s32 = c32_ref[...], s32_ref[...]

    sqt = _nt_dot(wsqt_ref[...], h)
    ct, st = c64t_ref[...], s64t_ref[...]
    half = HEAD_DIM // 2
    for hd in range(S_HEADS):
        x1 = sqt[hd * HEAD_DIM:hd * HEAD_DIM + half, :]
        x2 = sqt[hd * HEAD_DIM + half:(hd + 1) * HEAD_DIM, :]
        sqt_ref[0, hd * HEAD_DIM:hd * HEAD_DIM + half, :] = ((x1 * ct - x2 * st) * (HEAD_DIM ** -0.5)).astype(BF16)
        sqt_ref[0, hd * HEAD_DIM + half:(hd + 1) * HEAD_DIM, :] = ((x2 * ct + x1 * st) * (HEAD_DIM ** -0.5)).astype(BF16)
    sk_ref[0] = _rope_block(seg(OFF_SK, 128), c64, s64, first64, 32).astype(BF16)
    svt = _nt_dot(wsvt_ref[...], h)
    svt_feat = lax.broadcasted_iota(jnp.int32, svt.shape, 0) % LANES
    svt_ref[0] = jnp.where(svt_feat == HEAD_DIM, 1.0, svt).astype(BF16)

    dq = seg(OFF_DQ, 256)
    dk = seg(OFF_DK, 256)
    dq_scale = (D_QK ** -0.5) * LOG2E
    for c in range(2):
        sl = slice(c * LANES, (c + 1) * LANES)
        dq_ref[0, :, sl] = (_rope_block(dq[:, sl], c32, s32, first32, 16) * dq_scale).astype(BF16)
        dk_ref[0, :, sl] = _rope_block(dk[:, sl], c32, s32, first32, 16).astype(BF16)
    dvt = _nt_dot(wdvt_ref[...], h)
    feat = lax.broadcasted_iota(jnp.int32, dvt.shape, 0) % D_VX
    dvt_ref[0] = jnp.where(feat == D_V, 1.0, dvt).astype(BF16)


def _in_proj(x, g, w_packed, wg_t, wdv_t, wsq_t, wsv_t, tables, tm=512):
    B, S, _ = x.shape
    c64, s64, c32, s32, c64t, s64t = tables
    tab_t = pl.BlockSpec((HEAD_DIM // 2, tm), lambda si, b: (0, si))
    row = lambda width: pl.BlockSpec((1, tm, width), lambda si, b: (b, si, 0))
    const = lambda shape: pl.BlockSpec(shape, lambda si, b: (0,) * len(shape))
    tab = pl.BlockSpec((tm, LANES), lambda si, b: (si, 0))
    out_shape = [
        jax.ShapeDtypeStruct((B, S, 512), BF16),
        jax.ShapeDtypeStruct((B, S, 256), BF16),
        jax.ShapeDtypeStruct((B, S, 256), BF16),
        jax.ShapeDtypeStruct((B, S, 128), F32),
        jax.ShapeDtypeStruct((B, 8, S), F32),
        jax.ShapeDtypeStruct((B, 512, S), BF16),
        jax.ShapeDtypeStruct((B, S, 128), BF16),
        jax.ShapeDtypeStruct((B, 256, S), BF16),
        jax.ShapeDtypeStruct((B, S, 256), BF16),
        jax.ShapeDtypeStruct((B, S, 256), BF16),
        jax.ShapeDtypeStruct((B, D_HEADS * D_VX, S), BF16),
    ]
    out_specs = [row(512), row(256), row(256), row(128),
                 pl.BlockSpec((1, 8, tm), lambda si, b: (b, 0, si)),
                 pl.BlockSpec((1, 512, tm), lambda si, b: (b, 0, si)), row(128),
                 pl.BlockSpec((1, 256, tm), lambda si, b: (b, 0, si)), row(256), row(256),
                 pl.BlockSpec((1, D_HEADS * D_VX, tm), lambda si, b: (b, 0, si))]
    return pl.pallas_call(
        _in_proj_kernel,
        grid=(S // tm, B),
        in_specs=[row(D_MODEL), const((1, D_MODEL)), const((D_MODEL, PACKED_WIDTH)),
                  const((8, D_MODEL)), const((D_HEADS * D_VX, D_MODEL)), const((512, D_MODEL)), const((256, D_MODEL)),
                  tab, tab, tab, tab, tab_t, tab_t],
        out_specs=out_specs,
        out_shape=out_shape,
        scratch_shapes=[pltpu.VMEM((tm, D_MODEL), BF16)],
        compiler_params=pltpu.CompilerParams(
            dimension_semantics=("arbitrary", "arbitrary"), vmem_limit_bytes=VMEM_LIMIT),
        name="in_proj",
    )(x, g, w_packed, wg_t, wdv_t, wsq_t, wsv_t, c64, s64, c32, s32, c64t, s64t)


def _mlstm_kernel(qk_ref, v_ref, o_ref, gc_ref, gr_ref, cw_ref, cb_ref, bc_ref, br_ref, ngt_ref,
                  out_ref, xbuf, c_scr, m_scr, *, T):
    L = CHUNK

    @pl.when(pl.program_id(1) == 0)
    def _():
        xbuf[0:8, :] = jnp.zeros((8, 2 * M_WIDTH), F32)
        c_scr[...] = jnp.zeros_like(c_scr)
        m_scr[...] = jnp.zeros_like(m_scr)

    xbuf[8:8 + T, :] = qk_ref[0].astype(F32)
    y = cb_ref[...] + sum(cw_ref[j:j + 1, :] * xbuf[5 + j:5 + j + T, :] for j in range(CONV_WIDTH))
    xbuf[0:8, :] = xbuf[T:T + 8, :]
    act = y * _sigmoid(y)
    q_all = act[:, :M_WIDTH].astype(BF16)
    k_all = (act[:, M_WIDTH:] * (M_DIM ** -0.5)).astype(BF16)
    vt_all = v_ref[0].astype(F32).T

    gc = gc_ref[0] + bc_ref[...]
    gr = gr_ref[0] + br_ref[...]
    logf_c = _log_sigmoid(gc)
    logf_r = _log_sigmoid(gr)

    si = lax.broadcasted_iota(jnp.int32, (L, L), 0)
    ti = lax.broadcasted_iota(jnp.int32, (L, L), 1)
    keep = si <= ti
    tri = jnp.where(ti <= si, 1.0, 0.0).astype(BF16)
    tri_t = jnp.where(keep, 1.0, 0.0).astype(BF16)
    ones_rows = jnp.where(lax.broadcasted_iota(jnp.int32, (LANES - M_DIM, L), 0) == 0, 1.0, 0.0)

    for c in range(T // L):
        r0 = c * L
        cum_c = sum(_dot(tri, p) for p in _split3(logf_c[r0:r0 + L, :]))
        cum_r = sum(_dot(p, tri_t) for p in _split3(logf_r[:, r0:r0 + L]))
        ig_c = gc[r0:r0 + L, :]
        ig_r = gr[:, r0:r0 + L]

        staged = []
        for h in range(M_HEADS):
            b_row = cum_r[4 + h:5 + h, :]
            i_row = ig_r[h:h + 1, :]
            m_prev = m_scr[h, 0:1, 0:1]
            q_h = q_all[r0:r0 + L, h * M_DIM:(h + 1) * M_DIM]
            k_h = k_all[r0:r0 + L, h * M_DIM:(h + 1) * M_DIM]
            vt_ext = jnp.concatenate([vt_all[h * M_DIM:(h + 1) * M_DIM, r0:r0 + L], ones_rows], axis=0)

            b_last = b_row[:, L - 1:L]
            g_row = b_last - b_row + i_row
            m_new = jnp.maximum(b_last + m_prev, jnp.max(g_row, axis=1, keepdims=True))
            w_state = jnp.exp(g_row - m_new)
            decay = jnp.exp(b_last + m_prev - m_new)

            s_t = _nt_dot(k_h, q_h)
            c_old = c_scr[h]
            inter = _nt_dot(c_old.astype(BF16), q_h)
            c_scr[h] = decay * c_old + _dot((vt_ext * w_state).astype(BF16), k_h)
            m_scr[h] = jnp.broadcast_to(m_new, (8, LANES))
            staged.append((s_t, inter, vt_ext, b_row, m_prev))

        heads = []
        for h in range(M_HEADS):
            s_t, inter, vt_ext, b_row, m_prev = staged[h]
            col = ig_c[:, h:h + 1] - cum_c[:, 4 + h:5 + h]
            dmat = jnp.where(keep, b_row + col, -jnp.inf)
            m_t = jnp.maximum(b_row + m_prev, jnp.max(dmat, axis=0, keepdims=True))
            sc = (s_t * jnp.exp(dmat - m_t)).astype(BF16)
            a_inter = jnp.exp(b_row + m_prev - m_t)
            num = a_inter * inter + _dot(vt_ext.astype(BF16), sc)
            den = num[M_DIM:M_DIM + 1, :]
            hh = num[:M_DIM, :] * (1.0 / jnp.maximum(jnp.abs(den), jnp.exp(-m_t)))
            inv = lax.rsqrt(jnp.mean(hh * hh, axis=0, keepdims=True) + EPS)
            heads.append(hh * inv * ngt_ref[h * M_DIM:(h + 1) * M_DIM, :])
        hcat = jnp.concatenate(heads, axis=0).T
        gate = _sigmoid(o_ref[0, r0:r0 + L, :].astype(F32))
        out_ref[0, r0:r0 + L, :] = (gate * hcat).astype(BF16)


def _mlstm(qk, mv, mo, gate_c, gate_r, conv_w, conv_b, bias_c, bias_r, norm_g, T=256):
    B, S, _ = qk.shape
    row = lambda width: pl.BlockSpec((1, T, width), lambda b, t: (b, t, 0))
    const = lambda shape: pl.BlockSpec(shape, lambda b, t: (0,) * len(shape))
    return pl.pallas_call(
        functools.partial(_mlstm_kernel, T=T),
        grid=(B, S // T),
        in_specs=[row(512), row(256), row(256), row(128),
                  pl.BlockSpec((1, 8, T), lambda b, t: (b, 0, t)),
                  const((CONV_WIDTH, 2 * M_WIDTH)), const((1, 2 * M_WIDTH)),
                  const((1, LANES)), const((8, 1)), const((M_WIDTH, LANES))],
        out_specs=row(M_WIDTH),
        out_shape=jax.ShapeDtypeStruct((B, S, M_WIDTH), BF16),
        scratch_shapes=[pltpu.VMEM((T + 8, 2 * M_WIDTH), F32),
                        pltpu.VMEM((M_HEADS, LANES, M_DIM), F32),
                        pltpu.VMEM((M_HEADS, 8, LANES), F32)],
        compiler_params=pltpu.CompilerParams(
            dimension_semantics=("arbitrary", "arbitrary"), vmem_limit_bytes=VMEM_LIMIT),
        name="mlstm",
    )(qk, mv, mo, gate_c, gate_r, conv_w, conv_b, bias_c, bias_r, norm_g)


def _swa_kernel(sink_ref, qt_ref, kc_ref, kp_ref, vtc_ref, vtp_ref, out_ref, s_scr, *, TQ):
    W = WINDOW
    G = S_HEADS // S_KV_HEADS
    tile = pl.program_id(1)
    c = lax.broadcasted_iota(jnp.int32, (2 * W, G * W), 0)
    r = lax.broadcasted_iota(jnp.int32, (2 * W, G * W), 1) % W
    band = (c > r) & (c <= r + W)
    units = [(j, hk) for j in range(TQ // W) for hk in range(S_KV_HEADS)]
    n_slots = s_scr.shape[0]

    def scores(u):
        j, hk = units[u]
        if j == 0:
            kband = jnp.concatenate([kp_ref[0], kc_ref[0, 0:W, :]], axis=0)
        else:
            kband = kc_ref[0, (j - 1) * W:(j + 1) * W, :]
        qs = jnp.concatenate(
            [qt_ref[0, (hk * G + g) * HEAD_DIM:(hk * G + g + 1) * HEAD_DIM, j * W:(j + 1) * W] for g in range(G)],
            axis=1)
        zeros = jnp.zeros_like(qs)
        qpad = jnp.concatenate([qs, zeros] if hk == 0 else [zeros, qs], axis=0)
        s_scr[u % n_slots] = _dot(kband, qpad)

    def softmax_pv(u):
        j, hk = units[u]
        rows = slice(hk * LANES, (hk + 1) * LANES)
        if j == 0:
            vt = jnp.concatenate([vtp_ref[0, rows, :], vtc_ref[0, rows, 0:W]], axis=1)
            valid = band & ((tile > 0) | (c >= W))
        else:
            vt = vtc_ref[0, rows, (j - 1) * W:(j + 1) * W]
            valid = band
        sink = jnp.concatenate([jnp.full((1, W), sink_ref[hk * G + g], F32) for g in range(G)], axis=1)
        s = jnp.where(valid, s_scr[u % n_slots], -jnp.inf)
        mx = jnp.maximum(jnp.max(s, axis=0, keepdims=True), sink)
        e = jnp.exp(s - mx).astype(BF16)
        o = _dot(vt, e)
        denom = o[HEAD_DIM:HEAD_DIM + 1, :] + jnp.exp(sink - mx)
        o = o[:HEAD_DIM, :] * (1.0 / denom)
        return [o[:, g * W:(g + 1) * W] for g in range(G)]

    scores(0)
    scores(1)
    heads = []
    for u in range(len(units)):
        if u + 2 < len(units):
            scores(u + 2)
        heads.extend(softmax_pv(u))
        if len(heads) == S_HEADS:
            j = units[u][0]
            out_ref[0, j * W:(j + 1) * W, :] = jnp.concatenate(heads, axis=0).T.astype(BF16)
            heads = []


def _swa(sqt, sk, svt, sinks, TQ=512):
    B, _, S = sqt.shape
    per = TQ // WINDOW
    prev_blk = lambda t: jnp.maximum(t * per - 1, 0)
    return pl.pallas_call(
        functools.partial(_swa_kernel, TQ=TQ),
        grid=(B, S // TQ),
        in_specs=[pl.BlockSpec(memory_space=pltpu.SMEM),
                  pl.BlockSpec((1, S_WIDTH, TQ), lambda b, t: (b, 0, t)),
                  pl.BlockSpec((1, TQ, S_KV_WIDTH), lambda b, t: (b, t, 0)),
                  pl.BlockSpec((1, WINDOW, S_KV_WIDTH), lambda b, t: (b, prev_blk(t), 0)),
                  pl.BlockSpec((1, S_KV_HEADS * LANES, TQ), lambda b, t: (b, 0, t)),
                  pl.BlockSpec((1, S_KV_HEADS * LANES, WINDOW), lambda b, t: (b, 0, prev_blk(t)))],
        out_specs=pl.BlockSpec((1, TQ, S_WIDTH), lambda b, t: (b, t, 0)),
        out_shape=jax.ShapeDtypeStruct((B, S, S_WIDTH), BF16),
        scratch_shapes=[pltpu.VMEM((3, 2 * WINDOW, (S_HEADS // S_KV_HEADS) * WINDOW), F32)],
        compiler_params=pltpu.CompilerParams(
            dimension_semantics=("arbitrary", "arbitrary"), vmem_limit_bytes=VMEM_LIMIT),
        name="swa",
    )(sinks, sqt, sk, sk, svt, svt)


def _diff_kernel(lamp_ref, subg_ref, q_ref, k_ref, vt_ref, out_ref, s_scr, m_scr, acc_scr, *, TQ, TK, lam_init):
    qi = pl.program_id(2)
    q = q_ref[0]
    lane = lax.broadcasted_iota(jnp.int32, (TQ, LANES), 1)
    zero = jnp.zeros_like(q)
    qg = [jnp.where((lane >= g * D_QK) & (lane < (g + 1) * D_QK), q, zero) for g in range(4)]

    m_scr[...] = jnp.full_like(m_scr, -jnp.inf)
    acc_scr[...] = jnp.zeros_like(acc_scr)

    assert TQ == TK
    below_diag = (lax.broadcasted_iota(jnp.int32, (TK, TQ), 0)
                  <= lax.broadcasted_iota(jnp.int32, (TK, TQ), 1))

    def scores(j, g):
        start = pl.multiple_of(j * TK, TK)
        s_scr[g] = _nt_dot(k_ref[0, pl.ds(start, TK), :], qg[g])

    def softmax_pv(j, g, masked):
        start = pl.multiple_of(j * TK, TK)
        s = s_scr[g]
        if masked:
            s = jnp.where(below_diag, s, -jnp.inf)
        m_old = m_scr[g]
        m_new = jnp.maximum(m_old, jnp.max(s, axis=0, keepdims=True))
        p = jnp.exp2(s - m_new[0:1, :]).astype(BF16)
        alpha = jnp.exp2(m_old[0:1, :] - m_new[0:1, :])
        vt = vt_ref[0, (g // 2) * D_VX:(g // 2 + 1) * D_VX, pl.ds(start, TK)]
        acc_scr[g] = alpha * acc_scr[g] + _dot(vt, p)
        m_scr[g] = m_new

    def pipelined(j, masked, prefetch_next):
        scores(j, 2)
        softmax_pv(j, 0, masked)
        scores(j, 3)
        softmax_pv(j, 1, masked)
        if prefetch_next:
            scores(j + 1, 0)
        softmax_pv(j, 2, masked)
        if prefetch_next:
            scores(j + 1, 1)
        softmax_pv(j, 3, masked)

    n_full = (qi * TQ) // TK
    scores(0, 0)
    scores(0, 1)

    def pair_body(jj, carry):
        pipelined(2 * jj, False, True)
        pipelined(2 * jj + 1, False, True)
        return carry

    lax.fori_loop(0, n_full // 2, pair_body, 0)

    @pl.when(n_full % 2 == 1)
    def _():
        pipelined(n_full - 1, False, True)

    pipelined(n_full, True, False)

    lamp = lamp_ref[...]
    lam = (jnp.exp(jnp.sum(lamp[0:1] * lamp[1:2], axis=-1, keepdims=True))
           - jnp.exp(jnp.sum(lamp[2:3] * lamp[3:4], axis=-1, keepdims=True)) + lam_init)
    outs = []
    g_t = jnp.concatenate([subg_ref[...]] * (TQ // LANES), axis=1) * (1.0 - lam_init)
    for hd in range(2):
        a1 = acc_scr[2 * hd]
        a2 = acc_scr[2 * hd + 1]
        o = a1[:D_V, :] * (1.0 / a1[D_V:D_V + 1, :]) - lam * (a2[:D_V, :] * (1.0 / a2[D_V:D_V + 1, :]))
        outs.append(o * lax.rsqrt(jnp.mean(o * o, axis=0, keepdims=True) + EPS) * g_t)
    out_ref[0] = jnp.concatenate(outs, axis=0).T.astype(BF16)


def _diffattn(dq, dk, dvt, lamp, sub_g, lam_init, TQ=512, TK=512):
    B, S, _ = dq.shape
    return pl.pallas_call(
        functools.partial(_diff_kernel, TQ=TQ, TK=TK, lam_init=lam_init),
        grid=(B, D_HEADS // 2, S // TQ),
        in_specs=[pl.BlockSpec((4, LANES), lambda b, hp, i: (0, 0)),
                  pl.BlockSpec((D_V, LANES), lambda b, hp, i: (0, 0)),
                  pl.BlockSpec((1, TQ, LANES), lambda b, hp, i: (b, i, hp)),
                  pl.BlockSpec((1, S, LANES), lambda b, hp, i: (b, 0, hp)),
                  pl.BlockSpec((1, 2 * D_VX, S), lambda b, hp, i: (b, hp, 0))],
        out_specs=pl.BlockSpec((1, TQ, LANES), lambda b, hp, i: (b, i, hp)),
        out_shape=jax.ShapeDtypeStruct((B, S, D_WIDTH), BF16),
        scratch_shapes=[pltpu.VMEM((4, TK, TQ), F32), pltpu.VMEM((4, 8, TQ), F32),
                        pltpu.VMEM((4, D_VX, TQ), F32)],
        compiler_params=pltpu.CompilerParams(
            dimension_semantics=("arbitrary", "arbitrary", "arbitrary"), vmem_limit_bytes=VMEM_LIMIT),
        name="diffattn",
    )(lamp, sub_g, dq, dk, dvt)


def _out_mlp_kernel(x_ref, om_ref, os_ref, od_ref, wo_ref, wup_ref, wdn_ref,
                    gpm_ref, gpre_ref, gpost_ref, out_ref, *, FC):
    mix = (_dot(om_ref[...], wo_ref[0:M_WIDTH, :])
           + _dot(os_ref[...], wo_ref[M_WIDTH:M_WIDTH + S_WIDTH, :])
           + _dot(od_ref[...], wo_ref[M_WIDTH + S_WIDTH:, :]))
    x1 = x_ref[...] + _rms(mix, gpm_ref[...])
    h2 = _rms(x1, gpre_ref[...]).astype(BF16)
    acc = jnp.zeros(x1.shape, F32)
    for c in range(D_FF // FC):
        u = jnp.maximum(_dot(h2, wup_ref[:, c * FC:(c + 1) * FC]), 0.0)
        acc = acc + _dot((u * u).astype(BF16), wdn_ref[c * FC:(c + 1) * FC, :])
    out_ref[...] = x1 + _rms(acc, gpost_ref[...])


def _out_mlp(x2d, om, os_, od, w_out, w_up, w_down, g_post_mix, g_pre_mlp, g_post_mlp, tm=512, FC=1024):
    N = x2d.shape[0]
    row = lambda width: pl.BlockSpec((tm, width), lambda i: (i, 0))
    const = lambda shape: pl.BlockSpec(shape, lambda i: (0, 0), pipeline_mode=pl.Buffered(1))
    return pl.pallas_call(
        functools.partial(_out_mlp_kernel, FC=FC),
        grid=(N // tm,),
        in_specs=[row(D_MODEL), row(M_WIDTH), row(S_WIDTH), row(D_WIDTH),
                  const((D_MODEL, D_MODEL)), const((D_MODEL, D_FF)), const((D_FF, D_MODEL)),
                  const((1, D_MODEL)), const((1, D_MODEL)), const((1, D_MODEL))],
        out_specs=row(D_MODEL),
        out_shape=jax.ShapeDtypeStruct((N, D_MODEL), F32),
        compiler_params=pltpu.CompilerParams(
            dimension_semantics=("arbitrary",), vmem_limit_bytes=VMEM_LIMIT),
        name="out_mlp",
    )(x2d, om, os_, od, w_out, w_up, w_down, g_post_mix, g_pre_mlp, g_post_mlp)


def _rope_tables(S):
    pos = jnp.arange(S, dtype=F32)[:, None]
    lane = np.arange(LANES)

    def table(head_dim):
        half = head_dim // 2
        inv = ROPE_THETA ** (-jnp.arange(half, dtype=F32) * 2.0 / head_dim)
        ang = pos * inv[None, :]
        idx = (lane % head_dim) % half
        sign = np.where((lane % head_dim) < half, -1.0, 1.0).astype(np.float32)
        return jnp.cos(ang)[:, idx], jnp.sin(ang)[:, idx] * sign

    c64, s64 = table(HEAD_DIM)
    c32, s32 = table(D_QK)
    inv = ROPE_THETA ** (-jnp.arange(HEAD_DIM // 2, dtype=F32) * 2.0 / HEAD_DIM)
    ang_t = inv[:, None] * pos.T
    return c64, s64, c32, s32, jnp.cos(ang_t), jnp.sin(ang_t)


def _pack_w_in(w_in):
    sizes = (M_WIDTH, M_WIDTH, M_WIDTH, M_WIDTH, M_HEADS, M_HEADS,
             S_WIDTH, S_KV_WIDTH, S_KV_WIDTH, D_QK_WIDTH, D_QK_WIDTH, D_WIDTH)
    offs = np.concatenate([[0], np.cumsum(sizes)])
    part = [w_in[:, offs[i]:offs[i + 1]] for i in range(len(sizes))]
    mq, mk, mv, mo, mi, mf, sq, sk, sv, dq, dk, dv = part
    zeros = lambda n: jnp.zeros((D_MODEL, n), w_in.dtype)
    gates = jnp.concatenate([mi, mf, zeros(LANES - 2 * M_HEADS)], axis=1)
    dv_blocks = []
    for h in range(D_HEADS):
        dv_blocks += [dv[:, h * D_V:(h + 1) * D_V], zeros(D_VX - D_V)]
    sv_blocks = []
    for h in range(S_KV_HEADS):
        sv_blocks += [sv[:, h * HEAD_DIM:(h + 1) * HEAD_DIM], zeros(LANES - HEAD_DIM)]
    packed = jnp.concatenate([mq, mk, mv, mo, gates, sk, dq, dk], axis=1)
    wg_t = jnp.concatenate([mi, mf], axis=1).T
    wdv_t = jnp.concatenate(dv_blocks, axis=1).T
    wsv_t = jnp.concatenate(sv_blocks, axis=1).T
    return tuple(w.astype(BF16) for w in (packed, wg_t, wdv_t, sq.T, wsv_t))


def kernel(x, w_in, conv_w, conv_b, i_bias, f_bias, m_norm_g, sinks, lam_q1, lam_k1, lam_q2, lam_k2,
           sub_g, w_out, w_up, w_down, g_pre_mix, g_post_mix, g_pre_mlp, g_post_mlp):
    B, S, D = x.shape
    depth = w_in.shape[0]
    tables = _rope_tables(S)
    pad_lanes = lambda v: jnp.pad(v, (0, LANES - v.shape[0]))[None, :]
    lane_bcast = lambda v: jnp.broadcast_to(v[:, None], (v.shape[0], LANES))
    for l in range(depth):
        (qk, mv, mo, gate_c, gate_r, sqt, sk, svt, dq, dk, dvt) = _in_proj(
            x, g_pre_mix[l][None, :], *_pack_w_in(w_in[l]), tables)

        gate_bias = jnp.concatenate([i_bias[l], f_bias[l]])
        out_m = _mlstm(qk, mv, mo, gate_c, gate_r, conv_w[l], conv_b[l][None, :],
                       pad_lanes(gate_bias), gate_bias[:, None], lane_bcast(m_norm_g[l]))
        out_s = _swa(sqt, sk, svt, sinks[l])
        lamp = jnp.concatenate([pad_lanes(v) for v in (lam_q1[l], lam_k1[l], lam_q2[l], lam_k2[l])], axis=0)
        lam_init = 0.8 - 0.6 * math.exp(-0.3 * l)
        out_d = _diffattn(dq, dk, dvt, lamp, lane_bcast(sub_g[l]), lam_init)

        x = _out_mlp(x.reshape(B * S, D), out_m.reshape(B * S, -1), out_s.reshape(B * S, -1),
                     out_d.reshape(B * S, -1), w_out[l].astype(BF16), w_up[l].astype(BF16),
                     w_down[l].astype(BF16), g_post_mix[l][None, :], g_pre_mlp[l][None, :],
                     g_post_mlp[l][None, :]).reshape(B, S, D)
    return x
```

```python
import functools
import math

import jax
import jax.numpy as jnp
import numpy as np
from jax import lax
from jax.experimental import pallas as pl
from jax.experimental.pallas import tpu as pltpu

D_MODEL = 1024
M_HEADS = 4
M_DIM = 64
CONV_WIDTH = 4
CHUNK = 128
S_HEADS = 8
S_KV_HEADS = 2
HEAD_DIM = 64
WINDOW = 128
D_HEADS = 4
D_QK = 32
D_V = 64
ROPE_THETA = 10000.0
D_FF = 4 * D_MODEL
EPS = 1e-6

M_WIDTH = M_HEADS * M_DIM
S_WIDTH = S_HEADS * HEAD_DIM
S_KV_WIDTH = S_KV_HEADS * HEAD_DIM
D_QK_WIDTH = D_HEADS * 2 * D_QK
D_WIDTH = D_HEADS * D_V
D_VX = D_V + 16

LANES = 128
VMEM_LIMIT = 56 * 1024 * 1024

LOG2E = 1.4426950408889634

BF16 = jnp.bfloat16
F32 = jnp.float32

OFF_QK = 0
OFF_MV = 512
OFF_MO = 768
OFF_G = 1024
OFF_SK = 1152
OFF_DQ = 1280
OFF_DK = 1536
PACKED_WIDTH = 1792


def _nt_dot(a, b):
    return lax.dot_general(a, b, (((1,), (1,)), ((), ())), preferred_element_type=F32)


def _dot(a, b):
    return jnp.dot(a, b, preferred_element_type=F32)


def _rms(x, g):
    return x * lax.rsqrt(jnp.mean(x * x, axis=-1, keepdims=True) + EPS) * g


def _sigmoid(x):
    return 1.0 / (1.0 + jnp.exp(-x))


def _log_sigmoid(x):
    return jnp.minimum(x, 0.0) - jnp.log(1.0 + jnp.exp(-jnp.abs(x)))


def _split3(x):
    hi = x.astype(BF16)
    r1 = x - hi.astype(F32)
    mid = r1.astype(BF16)
    lo = (r1 - mid.astype(F32)).astype(BF16)
    return hi, mid, lo


def _rope_block(zb, cos, sin_signed, first_half, half):
    fwd = pltpu.roll(zb, LANES - half, 1)
    bwd = pltpu.roll(zb, half, 1)
    return zb * cos + jnp.where(first_half, fwd, bwd) * sin_signed


def _in_proj_kernel(x_ref, g_ref, w_ref, wgt_ref, wdvt_ref, wsqt_ref, wsvt_ref,
                    c64_ref, s64_ref, c32_ref, s32_ref, c64t_ref, s64t_ref,
                    qk_ref, mv_ref, mo_ref, gate_ref, gatet_ref,
                    sqt_ref, sk_ref, svt_ref, dq_ref, dk_ref, dvt_ref, *, PARTS):
    rows_per = x_ref.shape[1] // PARTS
    lane = lax.broadcasted_iota(jnp.int32, (rows_per, LANES), 1)
    first64 = (lane % 64) < 32
    first32 = (lane % 32) < 16
    half = HEAD_DIM // 2
    dq_scale = (D_QK ** -0.5) * LOG2E

    for p in range(PARTS):
        r = slice(p * rows_per, (p + 1) * rows_per)
        h = _rms(x_ref[0, r, :], g_ref[...]).astype(BF16)

        def seg(off, width):
            return _dot(h, w_ref[:, off:off + width])

        qk_ref[0, r, :] = seg(OFF_QK, 512).astype(BF16)
        mv_ref[0, r, :] = seg(OFF_MV, 256).astype(BF16)
        mo_ref[0, r, :] = seg(OFF_MO, 256).astype(BF16)
        gate_ref[0, r, :] = seg(OFF_G, 128)
        gatet_ref[0, :, r] = _nt_dot(wgt_ref[...], h)

        c64, s64 = c64_ref[r, :], s64_ref[r, :]
        c32, s32 = c32_ref[r, :], s32_ref[r, :]

        sqt = _nt_dot(wsqt_ref[...], h)
        ct, st = c64t_ref[:, r], s64t_ref[:, r]
        for hd in range(S_HEADS):
            x1 = sqt[hd * HEAD_DIM:hd * HEAD_DIM + half, :]
            x2 = sqt[hd * HEAD_DIM + half:(hd + 1) * HEAD_DIM, :]
            sqt_ref[0, hd * HEAD_DIM:hd * HEAD_DIM + half, r] = (
                (x1 * ct - x2 * st) * (HEAD_DIM ** -0.5)).astype(BF16)
            sqt_ref[0, hd * HEAD_DIM + half:(hd + 1) * HEAD_DIM, r] = (
                (x2 * ct + x1 * st) * (HEAD_DIM ** -0.5)).astype(BF16)
        sk_ref[0, r, :] = _rope_block(seg(OFF_SK, 128), c64, s64, first64, 32).astype(BF16)
        svt = _nt_dot(wsvt_ref[...], h)
        svt_feat = lax.broadcasted_iota(jnp.int32, svt.shape, 0) % LANES
        svt_ref[0, :, r] = jnp.where(svt_feat == HEAD_DIM, 1.0, svt).astype(BF16)

        dq = seg(OFF_DQ, 256)
        dk = seg(OFF_DK, 256)
        for c in range(2):
            sl = slice(c * LANES, (c + 1) * LANES)
            dq_ref[0, r, sl] = (_rope_block(dq[:, sl], c32, s32, first32, 16) * dq_scale).astype(BF16)
            dk_ref[0, r, sl] = _rope_block(dk[:, sl], c32, s32, first32, 16).astype(BF16)
        dvt = _nt_dot(wdvt_ref[...], h)
        feat = lax.broadcasted_iota(jnp.int32, dvt.shape, 0) % D_VX
        dvt_ref[0, :, r] = jnp.where(feat == D_V, 1.0, dvt).astype(BF16)


def _in_proj(x, g, w_packed, wg_t, wdv_t, wsq_t, wsv_t, tables, tm=1024, PARTS=4):
    B, S, _ = x.shape
    c64, s64, c32, s32, c64t, s64t = tables
    tab_t = pl.BlockSpec((HEAD_DIM // 2, tm), lambda si, b: (0, si))
    row = lambda width: pl.BlockSpec((1, tm, width), lambda si, b: (b, si, 0))
    const = lambda shape: pl.BlockSpec(shape, lambda si, b: (0,) * len(shape))
    tab = pl.BlockSpec((tm, LANES), lambda si, b: (si, 0))
    out_shape = [
        jax.ShapeDtypeStruct((B, S, 512), BF16),
        jax.ShapeDtypeStruct((B, S, 256), BF16),
        jax.ShapeDtypeStruct((B, S, 256), BF16),
        jax.ShapeDtypeStruct((B, S, 128), F32),
        jax.ShapeDtypeStruct((B, 8, S), F32),
        jax.ShapeDtypeStruct((B, 512, S), BF16),
        jax.ShapeDtypeStruct((B, S, 128), BF16),
        jax.ShapeDtypeStruct((B, 256, S), BF16),
        jax.ShapeDtypeStruct((B, S, 256), BF16),
        jax.ShapeDtypeStruct((B, S, 256), BF16),
        jax.ShapeDtypeStruct((B, D_HEADS * D_VX, S), BF16),
    ]
    out_specs = [row(512), row(256), row(256), row(128),
                 pl.BlockSpec((1, 8, tm), lambda si, b: (b, 0, si)),
                 pl.BlockSpec((1, 512, tm), lambda si, b: (b, 0, si)), row(128),
                 pl.BlockSpec((1, 256, tm), lambda si, b: (b, 0, si)), row(256), row(256),
                 pl.BlockSpec((1, D_HEADS * D_VX, tm), lambda si, b: (b, 0, si))]
    return pl.pallas_call(
        functools.partial(_in_proj_kernel, PARTS=PARTS),
        grid=(S // tm, B),
        in_specs=[row(D_MODEL), const((1, D_MODEL)), const((D_MODEL, PACKED_WIDTH)),
                  const((8, D_MODEL)), const((D_HEADS * D_VX, D_MODEL)), const((512, D_MODEL)), const((256, D_MODEL)),
                  tab, tab, tab, tab, tab_t, tab_t],
        out_specs=out_specs,
        out_shape=out_shape,
        compiler_params=pltpu.CompilerParams(
            dimension_semantics=("arbitrary", "arbitrary"), vmem_limit_bytes=VMEM_LIMIT),
        name="in_proj",
    )(x, g, w_packed, wg_t, wdv_t, wsq_t, wsv_t, c64, s64, c32, s32, c64t, s64t)


def _mlstm_kernel(qk_ref, v_ref, o_ref, gc_ref, gr_ref, cw_ref, cb_ref, bc_ref, br_ref, ngt_ref,
                  out_ref, xbuf, c_scr, m_scr, *, T):
    L = CHUNK

    @pl.when(pl.program_id(1) == 0)
    def _():
        xbuf[0:8, :] = jnp.zeros((8, 2 * M_WIDTH), F32)
        c_scr[...] = jnp.zeros_like(c_scr)
        m_scr[...] = jnp.zeros_like(m_scr)

    xbuf[8:8 + T, :] = qk_ref[0].astype(F32)
    y = cb_ref[...] + sum(cw_ref[j:j + 1, :] * xbuf[5 + j:5 + j + T, :] for j in range(CONV_WIDTH))
    xbuf[0:8, :] = xbuf[T:T + 8, :]
    act = y * _sigmoid(y)
    q_all = act[:, :M_WIDTH].astype(BF16)
    k_all = (act[:, M_WIDTH:] * (M_DIM ** -0.5)).astype(BF16)
    vt_all = v_ref[0].astype(F32).T

    gc = gc_ref[0] + bc_ref[...]
    gr = gr_ref[0] + br_ref[...]
    logf_c = _log_sigmoid(gc)
    logf_r = _log_sigmoid(gr)

    si = lax.broadcasted_iota(jnp.int32, (L, L), 0)
    ti = lax.broadcasted_iota(jnp.int32, (L, L), 1)
    keep = si <= ti
    tri = jnp.where(ti <= si, 1.0, 0.0).astype(BF16)
    tri_t = jnp.where(keep, 1.0, 0.0).astype(BF16)
    ones_rows = jnp.where(lax.broadcasted_iota(jnp.int32, (LANES - M_DIM, L), 0) == 0, 1.0, 0.0)

    for c in range(T // L):
        r0 = c * L
        cum_c = sum(_dot(tri, p) for p in _split3(logf_c[r0:r0 + L, :]))
        cum_r = sum(_dot(p, tri_t) for p in _split3(logf_r[:, r0:r0 + L]))
        ig_c = gc[r0:r0 + L, :]
        ig_r = gr[:, r0:r0 + L]

        staged = []
        for h in range(M_HEADS):
            b_row = cum_r[4 + h:5 + h, :]
            i_row = ig_r[h:h + 1, :]
            m_prev = m_scr[h, 0:1, 0:1]
            q_h = q_all[r0:r0 + L, h * M_DIM:(h + 1) * M_DIM]
            k_h = k_all[r0:r0 + L, h * M_DIM:(h + 1) * M_DIM]
            vt_ext = jnp.concatenate([vt_all[h * M_DIM:(h + 1) * M_DIM, r0:r0 + L], ones_rows], axis=0)

            b_last = b_row[:, L - 1:L]
            g_row = b_last - b_row + i_row
            m_new = jnp.maximum(b_last + m_prev, jnp.max(g_row, axis=1, keepdims=True))
            w_state = jnp.exp(g_row - m_new)
            decay = jnp.exp(b_last + m_prev - m_new)

            s_t = _nt_dot(k_h, q_h)
            c_old = c_scr[h]
            inter = _nt_dot(c_old.astype(BF16), q_h)
            c_scr[h] = decay * c_old + _dot((vt_ext * w_state).astype(BF16), k_h)
            m_scr[h] = jnp.broadcast_to(m_new, (8, LANES))
            staged.append((s_t, inter, vt_ext, b_row, m_prev))

        heads = []
        for h in range(M_HEADS):
            s_t, inter, vt_ext, b_row, m_prev = staged[h]
            col = ig_c[:, h:h + 1] - cum_c[:, 4 + h:5 + h]
            dmat = jnp.where(keep, b_row + col, -jnp.inf)
            m_t = jnp.maximum(b_row + m_prev, jnp.max(dmat, axis=0, keepdims=True))
            sc = (s_t * jnp.exp(dmat - m_t)).astype(BF16)
            a_inter = jnp.exp(b_row + m_prev - m_t)
            num = a_inter * inter + _dot(vt_ext.astype(BF16), sc)
            den = num[M_DIM:M_DIM + 1, :]
            hh = num[:M_DIM, :] * (1.0 / jnp.maximum(jnp.abs(den), jnp.exp(-m_t)))
            inv = lax.rsqrt(jnp.mean(hh * hh, axis=0, keepdims=True) + EPS)
            heads.append(hh * inv * ngt_ref[h * M_DIM:(h + 1) * M_DIM, :])
        hcat = jnp.concatenate(heads, axis=0).T
        gate = _sigmoid(o_ref[0, r0:r0 + L, :].astype(F32))
        out_ref[0, r0:r0 + L, :] = (gate * hcat).astype(BF16)


def _mlstm(qk, mv, mo, gate_c, gate_r, conv_w, conv_b, bias_c, bias_r, norm_g, T=256):
    B, S, _ = qk.shape
    row = lambda width: pl.BlockSpec((1, T, width), lambda b, t: (b, t, 0))
    const = lambda shape: pl.BlockSpec(shape, lambda b, t: (0,) * len(shape))
    return pl.pallas_call(
        functools.partial(_mlstm_kernel, T=T),
        grid=(B, S // T),
        in_specs=[row(512), row(256), row(256), row(128),
                  pl.BlockSpec((1, 8, T), lambda b, t: (b, 0, t)),
                  const((CONV_WIDTH, 2 * M_WIDTH)), const((1, 2 * M_WIDTH)),
                  const((1, LANES)), const((8, 1)), const((M_WIDTH, LANES))],
        out_specs=row(M_WIDTH),
        out_shape=jax.ShapeDtypeStruct((B, S, M_WIDTH), BF16),
        scratch_shapes=[pltpu.VMEM((T + 8, 2 * M_WIDTH), F32),
                        pltpu.VMEM((M_HEADS, LANES, M_DIM), F32),
                        pltpu.VMEM((M_HEADS, 8, LANES), F32)],
        compiler_params=pltpu.CompilerParams(
            dimension_semantics=("arbitrary", "arbitrary"), vmem_limit_bytes=VMEM_LIMIT),
        name="mlstm",
    )(qk, mv, mo, gate_c, gate_r, conv_w, conv_b, bias_c, bias_r, norm_g)


def _swa_kernel(sink_ref, qt_ref, kc_ref, kp_ref, vtc_ref, vtp_ref, out_ref, s_scr, *, TQ):
    W = WINDOW
    G = S_HEADS // S_KV_HEADS
    tile = pl.program_id(1)
    c = lax.broadcasted_iota(jnp.int32, (2 * W, G * W), 0)
    r = lax.broadcasted_iota(jnp.int32, (2 * W, G * W), 1) % W
    band = (c > r) & (c <= r + W)
    units = [(j, hk) for j in range(TQ // W) for hk in range(S_KV_HEADS)]
    n_slots = s_scr.shape[0]

    def scores(u):
        j, hk = units[u]
        if j == 0:
            kband = jnp.concatenate([kp_ref[0], kc_ref[0, 0:W, :]], axis=0)
        else:
            kband = kc_ref[0, (j - 1) * W:(j + 1) * W, :]
        qs = jnp.concatenate(
            [qt_ref[0, (hk * G + g) * HEAD_DIM:(hk * G + g + 1) * HEAD_DIM, j * W:(j + 1) * W] for g in range(G)],
            axis=1)
        zeros = jnp.zeros_like(qs)
        qpad = jnp.concatenate([qs, zeros] if hk == 0 else [zeros, qs], axis=0)
        s_scr[u % n_slots] = _dot(kband, qpad)

    def softmax_pv(u):
        j, hk = units[u]
        rows = slice(hk * LANES, (hk + 1) * LANES)
        if j == 0:
            vt = jnp.concatenate([vtp_ref[0, rows, :], vtc_ref[0, rows, 0:W]], axis=1)
            valid = band & ((tile > 0) | (c >= W))
        else:
            vt = vtc_ref[0, rows, (j - 1) * W:(j + 1) * W]
            valid = band
        sink = jnp.concatenate([jnp.full((1, W), sink_ref[hk * G + g], F32) for g in range(G)], axis=1)
        s = jnp.where(valid, s_scr[u % n_slots], -jnp.inf)
        mx = jnp.maximum(jnp.max(s, axis=0, keepdims=True), sink)
        e = jnp.exp(s - mx).astype(BF16)
        o = _dot(vt, e)
        denom = o[HEAD_DIM:HEAD_DIM + 1, :] + jnp.exp(sink - mx)
        o = o[:HEAD_DIM, :] * (1.0 / denom)
        return [o[:, g * W:(g + 1) * W] for g in range(G)]

    scores(0)
    scores(1)
    heads = []
    for u in range(len(units)):
        if u + 2 < len(units):
            scores(u + 2)
        heads.extend(softmax_pv(u))
        if len(heads) == S_HEADS:
            j = units[u][0]
            out_ref[0, j * W:(j + 1) * W, :] = jnp.concatenate(heads, axis=0).T.astype(BF16)
            heads = []


def _swa(sqt, sk, svt, sinks, TQ=512):
    B, _, S = sqt.shape
    per = TQ // WINDOW
    prev_blk = lambda t: jnp.maximum(t * per - 1, 0)
    return pl.pallas_call(
        functools.partial(_swa_kernel, TQ=TQ),
        grid=(B, S // TQ),
        in_specs=[pl.BlockSpec(memory_space=pltpu.SMEM),
                  pl.BlockSpec((1, S_WIDTH, TQ), lambda b, t: (b, 0, t)),
                  pl.BlockSpec((1, TQ, S_KV_WIDTH), lambda b, t: (b, t, 0)),
                  pl.BlockSpec((1, WINDOW, S_KV_WIDTH), lambda b, t: (b, prev_blk(t), 0)),
                  pl.BlockSpec((1, S_KV_HEADS * LANES, TQ), lambda b, t: (b, 0, t)),
                  pl.BlockSpec((1, S_KV_HEADS * LANES, WINDOW), lambda b, t: (b, 0, prev_blk(t)))],
        out_specs=pl.BlockSpec((1, TQ, S_WIDTH), lambda b, t: (b, t, 0)),
        out_shape=jax.ShapeDtypeStruct((B, S, S_WIDTH), BF16),
        scratch_shapes=[pltpu.VMEM((3, 2 * WINDOW, (S_HEADS // S_KV_HEADS) * WINDOW), F32)],
        compiler_params=pltpu.CompilerParams(
            dimension_semantics=("arbitrary", "arbitrary"), vmem_limit_bytes=VMEM_LIMIT),
        name="swa",
    )(sinks, sqt, sk, sk, svt, svt)


def _diff_kernel(lamp_ref, subg_ref, q_ref, k_ref, vt_ref, out_ref, s_scr, m_scr, acc_scr, *, TQ, TK, lam_init):
    qi = pl.program_id(2)
    q = q_ref[0]
    lane = lax.broadcasted_iota(jnp.int32, (TQ, LANES), 1)
    zero = jnp.zeros_like(q)
    qg = [jnp.where((lane >= g * D_QK) & (lane < (g + 1) * D_QK), q, zero) for g in range(4)]

    m_scr[...] = jnp.full_like(m_scr, -jnp.inf)
    acc_scr[...] = jnp.zeros_like(acc_scr)

    assert TQ == TK
    below_diag = (lax.broadcasted_iota(jnp.int32, (TK, TQ), 0)
                  <= lax.broadcasted_iota(jnp.int32, (TK, TQ), 1))

    def scores(j, g):
        start = pl.multiple_of(j * TK, TK)
        s_scr[g] = _nt_dot(k_ref[0, pl.ds(start, TK), :], qg[g])

    def softmax_pv(j, g, masked):
        start = pl.multiple_of(j * TK, TK)
        s = s_scr[g]
        if masked:
            s = jnp.where(below_diag, s, -jnp.inf)
        m_old = m_scr[g]
        m_new = jnp.maximum(m_old, jnp.max(s, axis=0, keepdims=True))
        p = jnp.exp2(s - m_new[0:1, :]).astype(BF16)
        alpha = jnp.exp2(m_old[0:1, :] - m_new[0:1, :])
        vt = vt_ref[0, (g // 2) * D_VX:(g // 2 + 1) * D_VX, pl.ds(start, TK)]
        acc_scr[g] = alpha * acc_scr[g] + _dot(vt, p)
        m_scr[g] = m_new

    def pipelined(j, masked, prefetch_next):
        scores(j, 2)
        softmax_pv(j, 0, masked)
        scores(j, 3)
        softmax_pv(j, 1, masked)
        if prefetch_next:
            scores(j + 1, 0)
        softmax_pv(j, 2, masked)
        if prefetch_next:
            scores(j + 1, 1)
        softmax_pv(j, 3, masked)

    n_full = (qi * TQ) // TK
    scores(0, 0)
    scores(0, 1)

    def pair_body(jj, carry):
        pipelined(2 * jj, False, True)
        pipelined(2 * jj + 1, False, True)
        return carry

    lax.fori_loop(0, n_full // 2, pair_body, 0)

    @pl.when(n_full % 2 == 1)
    def _():
        pipelined(n_full - 1, False, True)

    pipelined(n_full, True, False)

    lamp = lamp_ref[...]
    lam = (jnp.exp(jnp.sum(lamp[0:1] * lamp[1:2], axis=-1, keepdims=True))
           - jnp.exp(jnp.sum(lamp[2:3] * lamp[3:4], axis=-1, keepdims=True)) + lam_init)
    outs = []
    g_t = jnp.concatenate([subg_ref[...]] * (TQ // LANES), axis=1) * (1.0 - lam_init)
    for hd in range(2):
        a1 = acc_scr[2 * hd]
        a2 = acc_scr[2 * hd + 1]
        o = a1[:D_V, :] * (1.0 / a1[D_V:D_V + 1, :]) - lam * (a2[:D_V, :] * (1.0 / a2[D_V:D_V + 1, :]))
        outs.append(o * lax.rsqrt(jnp.mean(o * o, axis=0, keepdims=True) + EPS) * g_t)
    out_ref[0] = jnp.concatenate(outs, axis=0).T.astype(BF16)


def _diffattn(dq, dk, dvt, lamp, sub_g, lam_init, TQ=512, TK=512):
    B, S, _ = dq.shape
    return pl.pallas_call(
        functools.partial(_diff_kernel, TQ=TQ, TK=TK, lam_init=lam_init),
        grid=(B, D_HEADS // 2, S // TQ),
        in_specs=[pl.BlockSpec((4, LANES), lambda b, hp, i: (0, 0)),
                  pl.BlockSpec((D_V, LANES), lambda b, hp, i: (0, 0)),
                  pl.BlockSpec((1, TQ, LANES), lambda b, hp, i: (b, i, hp)),
                  pl.BlockSpec((1, S, LANES), lambda b, hp, i: (b, 0, hp)),
                  pl.BlockSpec((1, 2 * D_VX, S), lambda b, hp, i: (b, hp, 0))],
        out_specs=pl.BlockSpec((1, TQ, LANES), lambda b, hp, i: (b, i, hp)),
        out_shape=jax.ShapeDtypeStruct((B, S, D_WIDTH), BF16),
        scratch_shapes=[pltpu.VMEM((4, TK, TQ), F32), pltpu.VMEM((4, 8, TQ), F32),
                        pltpu.VMEM((4, D_VX, TQ), F32)],
        compiler_params=pltpu.CompilerParams(
            dimension_semantics=("arbitrary", "arbitrary", "arbitrary"), vmem_limit_bytes=VMEM_LIMIT),
        name="diffattn",
    )(lamp, sub_g, dq, dk, dvt)


def _out_mlp_kernel(x_ref, om_ref, os_ref, od_ref, wo_ref, wup_ref, wdn_ref,
                    gpm_ref, gpre_ref, gpost_ref, out_ref, *, FC, PARTS):
    half = x_ref.shape[0] // PARTS
    halves = [slice(p * half, (p + 1) * half) for p in range(PARTS)]

    def out_proj(rows):
        return (_dot(om_ref[rows, :], wo_ref[0:M_WIDTH, :])
                + _dot(os_ref[rows, :], wo_ref[M_WIDTH:M_WIDTH + S_WIDTH, :])
                + _dot(od_ref[rows, :], wo_ref[M_WIDTH + S_WIDTH:, :]))

    def mlp(h2):
        acc = jnp.zeros((half, D_MODEL), F32)
        for c in range(D_FF // FC):
            u = jnp.maximum(_dot(h2, wup_ref[:, c * FC:(c + 1) * FC]), 0.0)
            acc = acc + _dot((u * u).astype(BF16), wdn_ref[c * FC:(c + 1) * FC, :])
        return acc

    mixes = [out_proj(rows) for rows in halves]
    for rows, mix in zip(halves, mixes):
        x1 = x_ref[rows, :] + _rms(mix, gpm_ref[...])
        acc = mlp(_rms(x1, gpre_ref[...]).astype(BF16))
        out_ref[rows, :] = x1 + _rms(acc, gpost_ref[...])


def _out_mlp(x2d, om, os_, od, w_out, w_up, w_down, g_post_mix, g_pre_mlp, g_post_mlp, tm=1024, FC=1024, PARTS=4):
    N = x2d.shape[0]
    row = lambda width: pl.BlockSpec((tm, width), lambda i: (i, 0))
    const = lambda shape: pl.BlockSpec(shape, lambda i: (0, 0), pipeline_mode=pl.Buffered(1))
    return pl.pallas_call(
        functools.partial(_out_mlp_kernel, FC=FC, PARTS=PARTS),
        grid=(N // tm,),
        in_specs=[row(D_MODEL), row(M_WIDTH), row(S_WIDTH), row(D_WIDTH),
                  const((D_MODEL, D_MODEL)), const((D_MODEL, D_FF)), const((D_FF, D_MODEL)),
                  const((1, D_MODEL)), const((1, D_MODEL)), const((1, D_MODEL))],
        out_specs=row(D_MODEL),
        out_shape=jax.ShapeDtypeStruct((N, D_MODEL), F32),
        compiler_params=pltpu.CompilerParams(
            dimension_semantics=("arbitrary",), vmem_limit_bytes=VMEM_LIMIT),
        name="out_mlp",
    )(x2d, om, os_, od, w_out, w_up, w_down, g_post_mix, g_pre_mlp, g_post_mlp)


def _rope_tables(S):
    pos = jnp.arange(S, dtype=F32)[:, None]
    lane = np.arange(LANES)

    def table(head_dim):
        half = head_dim // 2
        inv = ROPE_THETA ** (-jnp.arange(half, dtype=F32) * 2.0 / head_dim)
        ang = pos * inv[None, :]
        idx = (lane % head_dim) % half
        sign = np.where((lane % head_dim) < half, -1.0, 1.0).astype(np.float32)
        return jnp.cos(ang)[:, idx], jnp.sin(ang)[:, idx] * sign

    c64, s64 = table(HEAD_DIM)
    c32, s32 = table(D_QK)
    inv = ROPE_THETA ** (-jnp.arange(HEAD_DIM // 2, dtype=F32) * 2.0 / HEAD_DIM)
    ang_t = inv[:, None] * pos.T
    return c64, s64, c32, s32, jnp.cos(ang_t), jnp.sin(ang_t)


def _pack_w_in(w_in):
    sizes = (M_WIDTH, M_WIDTH, M_WIDTH, M_WIDTH, M_HEADS, M_HEADS,
             S_WIDTH, S_KV_WIDTH, S_KV_WIDTH, D_QK_WIDTH, D_QK_WIDTH, D_WIDTH)
    offs = np.concatenate([[0], np.cumsum(sizes)])
    part = [w_in[:, offs[i]:offs[i + 1]] for i in range(len(sizes))]
    mq, mk, mv, mo, mi, mf, sq, sk, sv, dq, dk, dv = part
    zeros = lambda n: jnp.zeros((D_MODEL, n), w_in.dtype)
    gates = jnp.concatenate([mi, mf, zeros(LANES - 2 * M_HEADS)], axis=1)
    dv_blocks = []
    for h in range(D_HEADS):
        dv_blocks += [dv[:, h * D_V:(h + 1) * D_V], zeros(D_VX - D_V)]
    sv_blocks = []
    for h in range(S_KV_HEADS):
        sv_blocks += [sv[:, h * HEAD_DIM:(h + 1) * HEAD_DIM], zeros(LANES - HEAD_DIM)]
    packed = jnp.concatenate([mq, mk, mv, mo, gates, sk, dq, dk], axis=1)
    wg_t = jnp.concatenate([mi, mf], axis=1).T
    wdv_t = jnp.concatenate(dv_blocks, axis=1).T
    wsv_t = jnp.concatenate(sv_blocks, axis=1).T
    return tuple(w.astype(BF16) for w in (packed, wg_t, wdv_t, sq.T, wsv_t))


def kernel(x, w_in, conv_w, conv_b, i_bias, f_bias, m_norm_g, sinks, lam_q1, lam_k1, lam_q2, lam_k2,
           sub_g, w_out, w_up, w_down, g_pre_mix, g_post_mix, g_pre_mlp, g_post_mlp):
    B, S, D = x.shape
    depth = w_in.shape[0]
    tables = _rope_tables(S)
    pad_lanes = lambda v: jnp.pad(v, (0, LANES - v.shape[0]))[None, :]
    lane_bcast = lambda v: jnp.broadcast_to(v[:, None], (v.shape[0], LANES))
    for l in range(depth):
        (qk, mv, mo, gate_c, gate_r, sqt, sk, svt, dq, dk, dvt) = _in_proj(
            x, g_pre_mix[l][None, :], *_pack_w_in(w_in[l]), tables)

        gate_bias = jnp.concatenate([i_bias[l], f_bias[l]])
        out_m = _mlstm(qk, mv, mo, gate_c, gate_r, conv_w[l], conv_b[l][None, :],
                       pad_lanes(gate_bias), gate_bias[:, None], lane_bcast(m_norm_g[l]))
        out_s = _swa(sqt, sk, svt, sinks[l])
        lamp = jnp.concatenate([pad_lanes(v) for v in (lam_q1[l], lam_k1[l], lam_q2[l], lam_k2[l])], axis=0)
        lam_init = 0.8 - 0.6 * math.exp(-0.3 * l)
        out_d = _diffattn(dq, dk, dvt, lamp, lane_bcast(sub_g[l]), lam_init)

        x = _out_mlp(x.reshape(B * S, D), out_m.reshape(B * S, -1), out_s.reshape(B * S, -1),
                     out_d.reshape(B * S, -1), w_out[l].astype(BF16), w_up[l].astype(BF16),
                     w_down[l].astype(BF16), g_post_mix[l][None, :], g_pre_mlp[l][None, :],
                     g_post_mlp[l][None, :]).reshape(B, S, D)
    return x
```

```python
import functools
import math

import jax
import jax.numpy as jnp
import numpy as np
from jax import lax
from jax.experimental import pallas as pl
from jax.experimental.pallas import tpu as pltpu

D_MODEL = 1024
M_HEADS = 4
M_DIM = 64
CONV_WIDTH = 4
CHUNK = 128
S_HEADS = 8
S_KV_HEADS = 2
HEAD_DIM = 64
WINDOW = 128
D_HEADS = 4
D_QK = 32
D_V = 64
ROPE_THETA = 10000.0
D_FF = 4 * D_MODEL
EPS = 1e-6

M_WIDTH = M_HEADS * M_DIM
S_WIDTH = S_HEADS * HEAD_DIM
S_KV_WIDTH = S_KV_HEADS * HEAD_DIM
D_QK_WIDTH = D_HEADS * 2 * D_QK
D_WIDTH = D_HEADS * D_V
D_VX = D_V + 16

LANES = 128
VMEM_LIMIT = 56 * 1024 * 1024

LOG2E = 1.4426950408889634

BF16 = jnp.bfloat16
F32 = jnp.float32

OFF_QK = 0
OFF_MV = 512
OFF_MO = 768
OFF_SK = 1024
OFF_DQ = 1152
OFF_DK = 1408
PACKED_WIDTH = 1664


def _nt_dot(a, b):
    return lax.dot_general(a, b, (((1,), (1,)), ((), ())), preferred_element_type=F32)


def _dot(a, b):
    return jnp.dot(a, b, preferred_element_type=F32)


def _rms(x, g):
    return x * lax.rsqrt(jnp.mean(x * x, axis=-1, keepdims=True) + EPS) * g


def _sigmoid(x):
    return lax.logistic(x)


def _log_sigmoid(x):
    return jnp.minimum(x, 0.0) - jnp.log(1.0 + jnp.exp(-jnp.abs(x)))


def _split3(x):
    hi = x.astype(BF16)
    r1 = x - hi.astype(F32)
    mid = r1.astype(BF16)
    lo = (r1 - mid.astype(F32)).astype(BF16)
    return hi, mid, lo


def _rope_block(zb, cos, sin_signed, first_half, half):
    fwd = pltpu.roll(zb, LANES - half, 1)
    bwd = pltpu.roll(zb, half, 1)
    return zb * cos + jnp.where(first_half, fwd, bwd) * sin_signed


def _in_proj_kernel(x_ref, g_ref, w_ref, wgt_ref, wdvt_ref, wsqt_ref, wsvt_ref,
                    c64_ref, s64_ref, c32_ref, s32_ref, c64t_ref, s64t_ref,
                    qk_ref, mv_ref, mo_ref, gatet_ref,
                    sqt_ref, sk_ref, svt_ref, dq_ref, dk_ref, dvt_ref, *, PARTS):
    rows_per = x_ref.shape[1] // PARTS
    lane = lax.broadcasted_iota(jnp.int32, (rows_per, LANES), 1)
    first64 = (lane % 64) < 32
    first32 = (lane % 32) < 16
    half = HEAD_DIM // 2
    dq_scale = (D_QK ** -0.5) * LOG2E

    for p in range(PARTS):
        r = slice(p * rows_per, (p + 1) * rows_per)
        h = _rms(x_ref[0, r, :], g_ref[...]).astype(BF16)

        def seg(off, width):
            return _dot(h, w_ref[:, off:off + width])

        qk_ref[0, r, :] = seg(OFF_QK, 512).astype(BF16)
        mv_ref[0, r, :] = seg(OFF_MV, 256).astype(BF16)
        mo_ref[0, r, :] = seg(OFF_MO, 256).astype(BF16)
        gatet_ref[0, :, r] = _nt_dot(wgt_ref[...], h)

        c64, s64 = c64_ref[r, :], s64_ref[r, :]
        c32, s32 = c32_ref[r, :], s32_ref[r, :]

        sqt = _nt_dot(wsqt_ref[...], h)
        ct, st = c64t_ref[:, r], s64t_ref[:, r]
        for hd in range(S_HEADS):
            x1 = sqt[hd * HEAD_DIM:hd * HEAD_DIM + half, :]
            x2 = sqt[hd * HEAD_DIM + half:(hd + 1) * HEAD_DIM, :]
            sqt_ref[0, hd * HEAD_DIM:hd * HEAD_DIM + half, r] = (
                (x1 * ct - x2 * st) * (HEAD_DIM ** -0.5)).astype(BF16)
            sqt_ref[0, hd * HEAD_DIM + half:(hd + 1) * HEAD_DIM, r] = (
                (x2 * ct + x1 * st) * (HEAD_DIM ** -0.5)).astype(BF16)
        sk_ref[0, r, :] = _rope_block(seg(OFF_SK, 128), c64, s64, first64, 32).astype(BF16)
        svt = _nt_dot(wsvt_ref[...], h)
        svt_feat = lax.broadcasted_iota(jnp.int32, svt.shape, 0) % LANES
        svt_ref[0, :, r] = jnp.where(svt_feat == HEAD_DIM, 1.0, svt).astype(BF16)

        dq = seg(OFF_DQ, 256)
        dk = seg(OFF_DK, 256)
        for c in range(2):
            sl = slice(c * LANES, (c + 1) * LANES)
            dq_ref[0, r, sl] = (_rope_block(dq[:, sl], c32, s32, first32, 16) * dq_scale).astype(BF16)
            dk_ref[0, r, sl] = _rope_block(dk[:, sl], c32, s32, first32, 16).astype(BF16)
        dvt = _nt_dot(wdvt_ref[...], h)
        feat = lax.broadcasted_iota(jnp.int32, dvt.shape, 0) % D_VX
        dvt_ref[0, :, r] = jnp.where(feat == D_V, 1.0, dvt).astype(BF16)


def _in_proj(x, g, w_packed, wg_t, wdv_t, wsq_t, wsv_t, tables, tm=1024, PARTS=4):
    B, S, _ = x.shape
    c64, s64, c32, s32, c64t, s64t = tables
    tab_t = pl.BlockSpec((HEAD_DIM // 2, tm), lambda si, b: (0, si))
    row = lambda width: pl.BlockSpec((1, tm, width), lambda si, b: (b, si, 0))
    const = lambda shape: pl.BlockSpec(shape, lambda si, b: (0,) * len(shape))
    tab = pl.BlockSpec((tm, LANES), lambda si, b: (si, 0))
    out_shape = [
        jax.ShapeDtypeStruct((B, S, 512), BF16),
        jax.ShapeDtypeStruct((B, S, 256), BF16),
        jax.ShapeDtypeStruct((B, S, 256), BF16),
        jax.ShapeDtypeStruct((B, 8, S), F32),
        jax.ShapeDtypeStruct((B, 512, S), BF16),
        jax.ShapeDtypeStruct((B, S, 128), BF16),
        jax.ShapeDtypeStruct((B, 256, S), BF16),
        jax.ShapeDtypeStruct((B, S, 256), BF16),
        jax.ShapeDtypeStruct((B, S, 256), BF16),
        jax.ShapeDtypeStruct((B, D_HEADS * D_VX, S), BF16),
    ]
    out_specs = [row(512), row(256), row(256),
                 pl.BlockSpec((1, 8, tm), lambda si, b: (b, 0, si)),
                 pl.BlockSpec((1, 512, tm), lambda si, b: (b, 0, si)), row(128),
                 pl.BlockSpec((1, 256, tm), lambda si, b: (b, 0, si)), row(256), row(256),
                 pl.BlockSpec((1, D_HEADS * D_VX, tm), lambda si, b: (b, 0, si))]
    return pl.pallas_call(
        functools.partial(_in_proj_kernel, PARTS=PARTS),
        grid=(S // tm, B),
        in_specs=[row(D_MODEL), const((1, D_MODEL)), const((D_MODEL, PACKED_WIDTH)),
                  const((8, D_MODEL)), const((D_HEADS * D_VX, D_MODEL)), const((512, D_MODEL)), const((256, D_MODEL)),
                  tab, tab, tab, tab, tab_t, tab_t],
        out_specs=out_specs,
        out_shape=out_shape,
        compiler_params=pltpu.CompilerParams(
            dimension_semantics=("arbitrary", "arbitrary"), vmem_limit_bytes=VMEM_LIMIT),
        name="in_proj",
    )(x, g, w_packed, wg_t, wdv_t, wsq_t, wsv_t, c64, s64, c32, s32, c64t, s64t)


def _mlstm_kernel(qk_ref, v_ref, o_ref, gr_ref, cw_ref, cb_ref, br_ref, ngt_ref,
                  out_ref, xbuf, c_scr, m_scr, *, T):
    L = CHUNK

    @pl.when(pl.program_id(1) == 0)
    def _():
        xbuf[0:8, :] = jnp.zeros((8, 2 * M_WIDTH), F32)
        c_scr[...] = jnp.zeros_like(c_scr)
        m_scr[...] = jnp.zeros_like(m_scr)

    xbuf[8:8 + T, :] = qk_ref[0].astype(F32)
    y = cb_ref[...] + sum(cw_ref[j:j + 1, :] * xbuf[5 + j:5 + j + T, :] for j in range(CONV_WIDTH))
    xbuf[0:8, :] = xbuf[T:T + 8, :]
    act = y * _sigmoid(y)
    q_all = act[:, :M_WIDTH].astype(BF16)
    k_all = (act[:, M_WIDTH:] * (M_DIM ** -0.5)).astype(BF16)
    vt_all = v_ref[0].astype(F32).T

    gr = gr_ref[0] + br_ref[...]
    logf_r = _log_sigmoid(gr)

    si = lax.broadcasted_iota(jnp.int32, (L, L), 0)
    ti = lax.broadcasted_iota(jnp.int32, (L, L), 1)
    keep = si <= ti
    tri_t = jnp.where(keep, 1.0, 0.0).astype(BF16)
    ones_rows = jnp.where(lax.broadcasted_iota(jnp.int32, (LANES - M_DIM, L), 0) == 0, 1.0, 0.0)

    for c in range(T // L):
        r0 = c * L
        q_c, k_c = q_all[r0:r0 + L, :], k_all[r0:r0 + L, :]
        cum_r = sum(_dot(p, tri_t) for p in _split3(logf_r[:, r0:r0 + L]))
        ig_r = gr[:, r0:r0 + L]
        col_t = jnp.concatenate([pltpu.roll(ig_r, M_HEADS, 0) - cum_r,
                                 jnp.zeros((LANES - 8, L), F32)], axis=0).T

        staged = []
        for h in range(M_HEADS):
            b_row = cum_r[4 + h:5 + h, :]
            i_row = ig_r[h:h + 1, :]
            m_prev = m_scr[h, 0:1, 0:1]
            q_h = q_c[:, h * M_DIM:(h + 1) * M_DIM]
            k_h = k_c[:, h * M_DIM:(h + 1) * M_DIM]
            vt_ext = jnp.concatenate([vt_all[h * M_DIM:(h + 1) * M_DIM, r0:r0 + L], ones_rows], axis=0)

            b_last = b_row[:, L - 1:L]
            g_row = b_last - b_row + i_row
            m_new = jnp.maximum(b_last + m_prev, jnp.max(g_row, axis=1, keepdims=True))
            w_state = jnp.exp(g_row - m_new)
            decay = jnp.exp(b_last + m_prev - m_new)

            s_t = _nt_dot(k_h, q_h)
            c_old = c_scr[h]
            inter = _nt_dot(c_old.astype(BF16), q_h)
            c_scr[h] = decay * c_old + _dot((vt_ext * w_state).astype(BF16), k_h)
            m_scr[h] = jnp.broadcast_to(m_new, (8, LANES))
            staged.append((s_t, inter, vt_ext, b_row, m_prev))

        heads = []
        for h in range(M_HEADS):
            s_t, inter, vt_ext, b_row, m_prev = staged[h]
            col = col_t[:, 4 + h:5 + h]
            dmat = jnp.where(keep, b_row + col, -jnp.inf)
            m_t = jnp.maximum(b_row + m_prev, jnp.max(dmat, axis=0, keepdims=True))
            sc = (s_t * jnp.exp(dmat - m_t)).astype(BF16)
            a_inter = jnp.exp(b_row + m_prev - m_t)
            num = a_inter * inter + _dot(vt_ext.astype(BF16), sc)
            den = num[M_DIM:M_DIM + 1, :]
            hh = num[:M_DIM, :] * (1.0 / jnp.maximum(jnp.abs(den), jnp.exp(-m_t)))
            inv = lax.rsqrt(jnp.mean(hh * hh, axis=0, keepdims=True) + EPS)
            heads.append(hh * inv * ngt_ref[h * M_DIM:(h + 1) * M_DIM, :])
        hcat = jnp.concatenate(heads, axis=0).T
        gate = _sigmoid(o_ref[0, r0:r0 + L, :].astype(F32))
        out_ref[0, r0:r0 + L, :] = (gate * hcat).astype(BF16)


def _mlstm(qk, mv, mo, gate_r, conv_w, conv_b, bias_r, norm_g, T=1024):
    B, S, _ = qk.shape
    row = lambda width: pl.BlockSpec((1, T, width), lambda b, t: (b, t, 0))
    const = lambda shape: pl.BlockSpec(shape, lambda b, t: (0,) * len(shape))
    return pl.pallas_call(
        functools.partial(_mlstm_kernel, T=T),
        grid=(B, S // T),
        in_specs=[row(512), row(256), row(256),
                  pl.BlockSpec((1, 8, T), lambda b, t: (b, 0, t)),
                  const((CONV_WIDTH, 2 * M_WIDTH)), const((1, 2 * M_WIDTH)),
                  const((8, 1)), const((M_WIDTH, LANES))],
        out_specs=row(M_WIDTH),
        out_shape=jax.ShapeDtypeStruct((B, S, M_WIDTH), BF16),
        scratch_shapes=[pltpu.VMEM((T + 8, 2 * M_WIDTH), F32),
                        pltpu.VMEM((M_HEADS, LANES, M_DIM), F32),
                        pltpu.VMEM((M_HEADS, 8, LANES), F32)],
        compiler_params=pltpu.CompilerParams(
            dimension_semantics=("arbitrary", "arbitrary"), vmem_limit_bytes=VMEM_LIMIT),
        name="mlstm",
    )(qk, mv, mo, gate_r, conv_w, conv_b, bias_r, norm_g)


def _swa_kernel(sink_ref, qt_ref, kc_ref, kp_ref, vtc_ref, vtp_ref, out_ref, s_scr, *, TQ):
    W = WINDOW
    G = S_HEADS // S_KV_HEADS
    tile = pl.program_id(1)
    c = lax.broadcasted_iota(jnp.int32, (2 * W, G * W), 0)
    r = lax.broadcasted_iota(jnp.int32, (2 * W, G * W), 1) % W
    band = (c > r) & (c <= r + W)
    units = [(j, hk) for j in range(TQ // W) for hk in range(S_KV_HEADS)]
    n_slots = s_scr.shape[0]

    def scores(u):
        j, hk = units[u]
        if j == 0:
            kband = jnp.concatenate([kp_ref[0], kc_ref[0, 0:W, :]], axis=0)
        else:
            kband = kc_ref[0, (j - 1) * W:(j + 1) * W, :]
        qs = jnp.concatenate(
            [qt_ref[0, (hk * G + g) * HEAD_DIM:(hk * G + g + 1) * HEAD_DIM, j * W:(j + 1) * W] for g in range(G)],
            axis=1)
        zeros = jnp.zeros_like(qs)
        qpad = jnp.concatenate([qs, zeros] if hk == 0 else [zeros, qs], axis=0)
        s_scr[u % n_slots] = _dot(kband, qpad)

    def softmax_pv(u):
        j, hk = units[u]
        rows = slice(hk * LANES, (hk + 1) * LANES)
        if j == 0:
            vt = jnp.concatenate([vtp_ref[0, rows, :], vtc_ref[0, rows, 0:W]], axis=1)
            valid = band & ((tile > 0) | (c >= W))
        else:
            vt = vtc_ref[0, rows, (j - 1) * W:(j + 1) * W]
            valid = band
        sink = jnp.concatenate([jnp.full((1, W), sink_ref[hk * G + g], F32) for g in range(G)], axis=1)
        s = jnp.where(valid, s_scr[u % n_slots], -jnp.inf)
        mx = jnp.maximum(jnp.max(s, axis=0, keepdims=True), sink)
        e = jnp.exp(s - mx).astype(BF16)
        o = _dot(vt, e)
        denom = o[HEAD_DIM:HEAD_DIM + 1, :] + jnp.exp(sink - mx)
        o = o[:HEAD_DIM, :] * (1.0 / denom)
        return [o[:, g * W:(g + 1) * W] for g in range(G)]

    scores(0)
    scores(1)
    heads = []
    for u in range(len(units)):
        if u + 2 < len(units):
            scores(u + 2)
        heads.extend(softmax_pv(u))
        if len(heads) == S_HEADS:
            j = units[u][0]
            out_ref[0, j * W:(j + 1) * W, :] = jnp.concatenate(heads, axis=0).T.astype(BF16)
            heads = []


def _swa(sqt, sk, svt, sinks, TQ=512):
    B, _, S = sqt.shape
    per = TQ // WINDOW
    prev_blk = lambda t: jnp.maximum(t * per - 1, 0)
    return pl.pallas_call(
        functools.partial(_swa_kernel, TQ=TQ),
        grid=(B, S // TQ),
        in_specs=[pl.BlockSpec(memory_space=pltpu.SMEM),
                  pl.BlockSpec((1, S_WIDTH, TQ), lambda b, t: (b, 0, t)),
                  pl.BlockSpec((1, TQ, S_KV_WIDTH), lambda b, t: (b, t, 0)),
                  pl.BlockSpec((1, WINDOW, S_KV_WIDTH), lambda b, t: (b, prev_blk(t), 0)),
                  pl.BlockSpec((1, S_KV_HEADS * LANES, TQ), lambda b, t: (b, 0, t)),
                  pl.BlockSpec((1, S_KV_HEADS * LANES, WINDOW), lambda b, t: (b, 0, prev_blk(t)))],
        out_specs=pl.BlockSpec((1, TQ, S_WIDTH), lambda b, t: (b, t, 0)),
        out_shape=jax.ShapeDtypeStruct((B, S, S_WIDTH), BF16),
        scratch_shapes=[pltpu.VMEM((3, 2 * WINDOW, (S_HEADS // S_KV_HEADS) * WINDOW), F32)],
        compiler_params=pltpu.CompilerParams(
            dimension_semantics=("arbitrary", "arbitrary"), vmem_limit_bytes=VMEM_LIMIT),
        name="swa",
    )(sinks, sqt, sk, sk, svt, svt)


def _diff_kernel(lamp_ref, subg_ref, q_ref, k_ref, vt_ref, out_ref, s_scr, m_scr, acc_scr, *, TQ, TK, lam_init):
    qi = pl.program_id(2)
    q = q_ref[0]
    lane = lax.broadcasted_iota(jnp.int32, (TQ, LANES), 1)
    zero = jnp.zeros_like(q)
    qg = [jnp.where((lane >= g * D_QK) & (lane < (g + 1) * D_QK), q, zero) for g in range(4)]

    m_scr[...] = jnp.full_like(m_scr, -jnp.inf)
    acc_scr[...] = jnp.zeros_like(acc_scr)

    assert TQ == TK
    below_diag = (lax.broadcasted_iota(jnp.int32, (TK, TQ), 0)
                  <= lax.broadcasted_iota(jnp.int32, (TK, TQ), 1))

    def scores(j, g):
        start = pl.multiple_of(j * TK, TK)
        s_scr[g] = _nt_dot(k_ref[0, pl.ds(start, TK), :], qg[g])

    def softmax_pv(j, g, masked):
        start = pl.multiple_of(j * TK, TK)
        s = s_scr[g]
        if masked:
            s = jnp.where(below_diag, s, -jnp.inf)
        m_old = m_scr[g]
        m_new = jnp.maximum(m_old, jnp.max(s, axis=0, keepdims=True))
        p = jnp.exp2(s - m_new[0:1, :]).astype(BF16)
        alpha = jnp.exp2(m_old[0:1, :] - m_new[0:1, :])
        vt = vt_ref[0, (g // 2) * D_VX:(g // 2 + 1) * D_VX, pl.ds(start, TK)]
        acc_scr[g] = alpha * acc_scr[g] + _dot(vt, p)
        m_scr[g] = m_new

    def pipelined(j, masked, prefetch_next):
        scores(j, 2)
        softmax_pv(j, 0, masked)
        scores(j, 3)
        softmax_pv(j, 1, masked)
        if prefetch_next:
            scores(j + 1, 0)
        softmax_pv(j, 2, masked)
        if prefetch_next:
            scores(j + 1, 1)
        softmax_pv(j, 3, masked)

    n_full = (qi * TQ) // TK
    scores(0, 0)
    scores(0, 1)

    def pair_body(jj, carry):
        pipelined(2 * jj, False, True)
        pipelined(2 * jj + 1, False, True)
        return carry

    lax.fori_loop(0, n_full // 2, pair_body, 0)

    @pl.when(n_full % 2 == 1)
    def _():
        pipelined(n_full - 1, False, True)

    pipelined(n_full, True, False)

    lamp = lamp_ref[...]
    lam = (jnp.exp(jnp.sum(lamp[0:1] * lamp[1:2], axis=-1, keepdims=True))
           - jnp.exp(jnp.sum(lamp[2:3] * lamp[3:4], axis=-1, keepdims=True)) + lam_init)
    outs = []
    g_t = jnp.concatenate([subg_ref[...]] * (TQ // LANES), axis=1) * (1.0 - lam_init)
    for hd in range(2):
        a1 = acc_scr[2 * hd]
        a2 = acc_scr[2 * hd + 1]
        o = a1[:D_V, :] * (1.0 / a1[D_V:D_V + 1, :]) - lam * (a2[:D_V, :] * (1.0 / a2[D_V:D_V + 1, :]))
        outs.append(o * lax.rsqrt(jnp.mean(o * o, axis=0, keepdims=True) + EPS) * g_t)
    out_ref[0] = jnp.concatenate(outs, axis=0).T.astype(BF16)


def _diffattn(dq, dk, dvt, lamp, sub_g, lam_init, TQ=512, TK=512):
    B, S, _ = dq.shape
    return pl.pallas_call(
        functools.partial(_diff_kernel, TQ=TQ, TK=TK, lam_init=lam_init),
        grid=(B, D_HEADS // 2, S // TQ),
        in_specs=[pl.BlockSpec((4, LANES), lambda b, hp, i: (0, 0)),
                  pl.BlockSpec((D_V, LANES), lambda b, hp, i: (0, 0)),
                  pl.BlockSpec((1, TQ, LANES), lambda b, hp, i: (b, i, hp)),
                  pl.BlockSpec((1, S, LANES), lambda b, hp, i: (b, 0, hp)),
                  pl.BlockSpec((1, 2 * D_VX, S), lambda b, hp, i: (b, hp, 0))],
        out_specs=pl.BlockSpec((1, TQ, LANES), lambda b, hp, i: (b, i, hp)),
        out_shape=jax.ShapeDtypeStruct((B, S, D_WIDTH), BF16),
        scratch_shapes=[pltpu.VMEM((4, TK, TQ), F32), pltpu.VMEM((4, 8, TQ), F32),
                        pltpu.VMEM((4, D_VX, TQ), F32)],
        compiler_params=pltpu.CompilerParams(
            dimension_semantics=("arbitrary", "arbitrary", "arbitrary"), vmem_limit_bytes=VMEM_LIMIT),
        name="diffattn",
    )(lamp, sub_g, dq, dk, dvt)


def _out_mlp_kernel(x_ref, om_ref, os_ref, od_ref, wo_ref, wup_ref, wdn_ref,
                    gpm_ref, gpre_ref, gpost_ref, out_ref, *, FC, PARTS):
    half = x_ref.shape[0] // PARTS
    halves = [slice(p * half, (p + 1) * half) for p in range(PARTS)]

    def out_proj(rows):
        return (_dot(om_ref[rows, :], wo_ref[0:M_WIDTH, :])
                + _dot(os_ref[rows, :], wo_ref[M_WIDTH:M_WIDTH + S_WIDTH, :])
                + _dot(od_ref[rows, :], wo_ref[M_WIDTH + S_WIDTH:, :]))

    def mlp(h2):
        acc = jnp.zeros((half, D_MODEL), F32)
        for c in range(D_FF // FC):
            u = jnp.maximum(_dot(h2, wup_ref[:, c * FC:(c + 1) * FC]), 0.0)
            acc = acc + _dot((u * u).astype(BF16), wdn_ref[c * FC:(c + 1) * FC, :])
        return acc

    mixes = [out_proj(rows) for rows in halves]
    for rows, mix in zip(halves, mixes):
        x1 = x_ref[rows, :] + _rms(mix, gpm_ref[...])
        acc = mlp(_rms(x1, gpre_ref[...]).astype(BF16))
        out_ref[rows, :] = x1 + _rms(acc, gpost_ref[...])


def _out_mlp(x2d, om, os_, od, w_out, w_up, w_down, g_post_mix, g_pre_mlp, g_post_mlp, tm=1024, FC=1024, PARTS=4):
    N = x2d.shape[0]
    row = lambda width: pl.BlockSpec((tm, width), lambda i: (i, 0))
    const = lambda shape: pl.BlockSpec(shape, lambda i: (0, 0), pipeline_mode=pl.Buffered(1))
    return pl.pallas_call(
        functools.partial(_out_mlp_kernel, FC=FC, PARTS=PARTS),
        grid=(N // tm,),
        in_specs=[row(D_MODEL), row(M_WIDTH), row(S_WIDTH), row(D_WIDTH),
                  const((D_MODEL, D_MODEL)), const((D_MODEL, D_FF)), const((D_FF, D_MODEL)),
                  const((1, D_MODEL)), const((1, D_MODEL)), const((1, D_MODEL))],
        out_specs=row(D_MODEL),
        out_shape=jax.ShapeDtypeStruct((N, D_MODEL), F32),
        compiler_params=pltpu.CompilerParams(
            dimension_semantics=("arbitrary",), vmem_limit_bytes=VMEM_LIMIT),
        name="out_mlp",
    )(x2d, om, os_, od, w_out, w_up, w_down, g_post_mix, g_pre_mlp, g_post_mlp)


def _rope_tables(S):
    pos = jnp.arange(S, dtype=F32)[:, None]
    lane = np.arange(LANES)

    def table(head_dim):
        half = head_dim // 2
        inv = ROPE_THETA ** (-jnp.arange(half, dtype=F32) * 2.0 / head_dim)
        ang = pos * inv[None, :]
        idx = (lane % head_dim) % half
        sign = np.where((lane % head_dim) < half, -1.0, 1.0).astype(np.float32)
        return jnp.cos(ang)[:, idx], jnp.sin(ang)[:, idx] * sign

    c64, s64 = table(HEAD_DIM)
    c32, s32 = table(D_QK)
    inv = ROPE_THETA ** (-jnp.arange(HEAD_DIM // 2, dtype=F32) * 2.0 / HEAD_DIM)
    ang_t = inv[:, None] * pos.T
    return c64, s64, c32, s32, jnp.cos(ang_t), jnp.sin(ang_t)


def _pack_w_in(w_in):
    sizes = (M_WIDTH, M_WIDTH, M_WIDTH, M_WIDTH, M_HEADS, M_HEADS,
             S_WIDTH, S_KV_WIDTH, S_KV_WIDTH, D_QK_WIDTH, D_QK_WIDTH, D_WIDTH)
    offs = np.concatenate([[0], np.cumsum(sizes)])
    part = [w_in[:, offs[i]:offs[i + 1]] for i in range(len(sizes))]
    mq, mk, mv, mo, mi, mf, sq, sk, sv, dq, dk, dv = part
    zeros = lambda n: jnp.zeros((D_MODEL, n), w_in.dtype)
    dv_blocks = []
    for h in range(D_HEADS):
        dv_blocks += [dv[:, h * D_V:(h + 1) * D_V], zeros(D_VX - D_V)]
    sv_blocks = []
    for h in range(S_KV_HEADS):
        sv_blocks += [sv[:, h * HEAD_DIM:(h + 1) * HEAD_DIM], zeros(LANES - HEAD_DIM)]
    packed = jnp.concatenate([mq, mk, mv, mo, sk, dq, dk], axis=1)
    wg_t = jnp.concatenate([mi, mf], axis=1).T
    wdv_t = jnp.concatenate(dv_blocks, axis=1).T
    wsv_t = jnp.concatenate(sv_blocks, axis=1).T
    return tuple(w.astype(BF16) for w in (packed, wg_t, wdv_t, sq.T, wsv_t))


def kernel(x, w_in, conv_w, conv_b, i_bias, f_bias, m_norm_g, sinks, lam_q1, lam_k1, lam_q2, lam_k2,
           sub_g, w_out, w_up, w_down, g_pre_mix, g_post_mix, g_pre_mlp, g_post_mlp):
    B, S, D = x.shape
    depth = w_in.shape[0]
    tables = _rope_tables(S)
    pad_lanes = lambda v: jnp.pad(v, (0, LANES - v.shape[0]))[None, :]
    lane_bcast = lambda v: jnp.broadcast_to(v[:, None], (v.shape[0], LANES))
    for l in range(depth):
        (qk, mv, mo, gate_r, sqt, sk, svt, dq, dk, dvt) = _in_proj(
            x, g_pre_mix[l][None, :], *_pack_w_in(w_in[l]), tables)

        gate_bias = jnp.concatenate([i_bias[l], f_bias[l]])
        out_m = _mlstm(qk, mv, mo, gate_r, conv_w[l], conv_b[l][None, :],
                       gate_bias[:, None], lane_bcast(m_norm_g[l]))
        out_s = _swa(sqt, sk, svt, sinks[l])
        lamp = jnp.concatenate([pad_lanes(v) for v in (lam_q1[l], lam_k1[l], lam_q2[l], lam_k2[l])], axis=0)
        lam_init = 0.8 - 0.6 * math.exp(-0.3 * l)
        out_d = _diffattn(dq, dk, dvt, lamp, lane_bcast(sub_g[l]), lam_init)

        x = _out_mlp(x.reshape(B * S, D), out_m.reshape(B * S, -1), out_s.reshape(B * S, -1),
                     out_d.reshape(B * S, -1), w_out[l].astype(BF16), w_up[l].astype(BF16),
                     w_down[l].astype(BF16), g_post_mix[l][None, :], g_pre_mlp[l][None, :],
                     g_post_mlp[l][None, :]).reshape(B, S, D)
    return x
```

```python
import functools
import math

import jax
import jax.numpy as jnp
import numpy as np
from jax import lax
from jax.experimental import pallas as pl
from jax.experimental.pallas import tpu as pltpu

D_MODEL = 1024
M_HEADS = 4
M_DIM = 64
CONV_WIDTH = 4
CHUNK = 128
S_HEADS = 8
S_KV_HEADS = 2
HEAD_DIM = 64
WINDOW = 128
D_HEADS = 4
D_QK = 32
D_V = 64
ROPE_THETA = 10000.0
D_FF = 4 * D_MODEL
EPS = 1e-6

M_WIDTH = M_HEADS * M_DIM
S_WIDTH = S_HEADS * HEAD_DIM
S_KV_WIDTH = S_KV_HEADS * HEAD_DIM
D_QK_WIDTH = D_HEADS * 2 * D_QK
D_WIDTH = D_HEADS * D_V
D_VX = D_V + 16

LANES = 128
VMEM_LIMIT = 56 * 1024 * 1024

LOG2E = 1.4426950408889634

BF16 = jnp.bfloat16
F32 = jnp.float32

OFF_QK = 0
OFF_MV = 512
OFF_MO = 768
OFF_SK = 1024
OFF_DQ = 1152
OFF_DK = 1408
PACKED_WIDTH = 1664


def _nt_dot(a, b):
    return lax.dot_general(a, b, (((1,), (1,)), ((), ())), preferred_element_type=F32)


def _dot(a, b):
    return jnp.dot(a, b, preferred_element_type=F32)


def _rms(x, g):
    return x * lax.rsqrt(jnp.mean(x * x, axis=-1, keepdims=True) + EPS) * g


def _sigmoid(x):
    return lax.logistic(x)


def _log_sigmoid(x):
    return jnp.minimum(x, 0.0) - jnp.log(1.0 + jnp.exp(-jnp.abs(x)))


def _split3(x):
    hi = x.astype(BF16)
    r1 = x - hi.astype(F32)
    mid = r1.astype(BF16)
    lo = (r1 - mid.astype(F32)).astype(BF16)
    return hi, mid, lo


def _rope_block(zb, cos, sin_signed, first_half, half):
    fwd = pltpu.roll(zb, LANES - half, 1)
    bwd = pltpu.roll(zb, half, 1)
    return zb * cos + jnp.where(first_half, fwd, bwd) * sin_signed


def _in_proj_kernel(x_ref, g_ref, w_ref, wgt_ref, wdvt_ref, wsqt_ref, wsvt_ref,
                    c64_ref, s64_ref, c32_ref, s32_ref, c64t_ref, s64t_ref,
                    qk_ref, mv_ref, mo_ref, gatet_ref,
                    sqt_ref, sk_ref, svt_ref, dq_ref, dk_ref, dvt_ref, *, PARTS):
    rows_per = x_ref.shape[1] // PARTS
    lane = lax.broadcasted_iota(jnp.int32, (rows_per, LANES), 1)
    first64 = (lane % 64) < 32
    first32 = (lane % 32) < 16
    half = HEAD_DIM // 2
    dq_scale = (D_QK ** -0.5) * LOG2E

    for p in range(PARTS):
        r = slice(p * rows_per, (p + 1) * rows_per)
        h = _rms(x_ref[0, r, :], g_ref[...]).astype(BF16)

        def seg(off, width):
            return _dot(h, w_ref[:, off:off + width])

        qk_ref[0, r, :] = seg(OFF_QK, 512).astype(BF16)
        mv_ref[0, r, :] = seg(OFF_MV, 256).astype(BF16)
        mo_ref[0, r, :] = seg(OFF_MO, 256).astype(BF16)
        gatet_ref[0, :, r] = _nt_dot(wgt_ref[...], h)

        c64, s64 = c64_ref[r, :], s64_ref[r, :]
        c32, s32 = c32_ref[r, :], s32_ref[r, :]

        sqt = _nt_dot(wsqt_ref[...], h)
        ct, st = c64t_ref[:, r], s64t_ref[:, r]
        for hd in range(S_HEADS):
            x1 = sqt[hd * HEAD_DIM:hd * HEAD_DIM + half, :]
            x2 = sqt[hd * HEAD_DIM + half:(hd + 1) * HEAD_DIM, :]
            sqt_ref[0, hd * HEAD_DIM:hd * HEAD_DIM + half, r] = (
                (x1 * ct - x2 * st) * (HEAD_DIM ** -0.5)).astype(BF16)
            sqt_ref[0, hd * HEAD_DIM + half:(hd + 1) * HEAD_DIM, r] = (
                (x2 * ct + x1 * st) * (HEAD_DIM ** -0.5)).astype(BF16)
        sk_ref[0, r, :] = _rope_block(seg(OFF_SK, 128), c64, s64, first64, 32).astype(BF16)
        svt = _nt_dot(wsvt_ref[...], h)
        svt_feat = lax.broadcasted_iota(jnp.int32, svt.shape, 0) % LANES
        svt_ref[0, :, r] = jnp.where(svt_feat == HEAD_DIM, 1.0, svt).astype(BF16)

        dq = seg(OFF_DQ, 256)
        dk = seg(OFF_DK, 256)
        for c in range(2):
            sl = slice(c * LANES, (c + 1) * LANES)
            dq_ref[0, r, sl] = (_rope_block(dq[:, sl], c32, s32, first32, 16) * dq_scale).astype(BF16)
            dk_ref[0, r, sl] = _rope_block(dk[:, sl], c32, s32, first32, 16).astype(BF16)
        dvt = _nt_dot(wdvt_ref[...], h)
        feat = lax.broadcasted_iota(jnp.int32, dvt.shape, 0) % D_VX
        dvt_ref[0, :, r] = jnp.where(feat == D_V, 1.0, dvt).astype(BF16)


def _in_proj(x, g, w_packed, wg_t, wdv_t, wsq_t, wsv_t, tables, tm=1024, PARTS=4):
    B, S, _ = x.shape
    c64, s64, c32, s32, c64t, s64t = tables
    tab_t = pl.BlockSpec((HEAD_DIM // 2, tm), lambda si, b: (0, si))
    row = lambda width: pl.BlockSpec((1, tm, width), lambda si, b: (b, si, 0))
    const = lambda shape: pl.BlockSpec(shape, lambda si, b: (0,) * len(shape))
    tab = pl.BlockSpec((tm, LANES), lambda si, b: (si, 0))
    out_shape = [
        jax.ShapeDtypeStruct((B, S, 512), BF16),
        jax.ShapeDtypeStruct((B, S, 256), BF16),
        jax.ShapeDtypeStruct((B, S, 256), BF16),
        jax.ShapeDtypeStruct((B, 8, S), F32),
        jax.ShapeDtypeStruct((B, 512, S), BF16),
        jax.ShapeDtypeStruct((B, S, 128), BF16),
        jax.ShapeDtypeStruct((B, 256, S), BF16),
        jax.ShapeDtypeStruct((B, S, 256), BF16),
        jax.ShapeDtypeStruct((B, S, 256), BF16),
        jax.ShapeDtypeStruct((B, D_HEADS * D_VX, S), BF16),
    ]
    out_specs = [row(512), row(256), row(256),
                 pl.BlockSpec((1, 8, tm), lambda si, b: (b, 0, si)),
                 pl.BlockSpec((1, 512, tm), lambda si, b: (b, 0, si)), row(128),
                 pl.BlockSpec((1, 256, tm), lambda si, b: (b, 0, si)), row(256), row(256),
                 pl.BlockSpec((1, D_HEADS * D_VX, tm), lambda si, b: (b, 0, si))]
    return pl.pallas_call(
        functools.partial(_in_proj_kernel, PARTS=PARTS),
        grid=(S // tm, B),
        in_specs=[row(D_MODEL), const((1, D_MODEL)), const((D_MODEL, PACKED_WIDTH)),
                  const((8, D_MODEL)), const((D_HEADS * D_VX, D_MODEL)), const((512, D_MODEL)), const((256, D_MODEL)),
                  tab, tab, tab, tab, tab_t, tab_t],
        out_specs=out_specs,
        out_shape=out_shape,
        compiler_params=pltpu.CompilerParams(
            dimension_semantics=("arbitrary", "arbitrary"), vmem_limit_bytes=VMEM_LIMIT),
        name="in_proj",
    )(x, g, w_packed, wg_t, wdv_t, wsq_t, wsv_t, c64, s64, c32, s32, c64t, s64t)


def _mlstm_kernel(qk_ref, v_ref, o_ref, gr_ref, cw_ref, cb_ref, br_ref, ngt_ref,
                  out_ref, xbuf, c_scr, m_scr, *, T):
    L = CHUNK

    @pl.when(pl.program_id(1) == 0)
    def _():
        xbuf[0:8, :] = jnp.zeros((8, 2 * M_WIDTH), F32)
        c_scr[...] = jnp.zeros_like(c_scr)
        m_scr[...] = jnp.zeros_like(m_scr)

    xbuf[8:8 + T, :] = qk_ref[0].astype(F32)
    y = cb_ref[...] + sum(cw_ref[j:j + 1, :] * xbuf[5 + j:5 + j + T, :] for j in range(CONV_WIDTH))
    xbuf[0:8, :] = xbuf[T:T + 8, :]
    act = y * _sigmoid(y)
    q_all = act[:, :M_WIDTH].astype(BF16)
    k_all = (act[:, M_WIDTH:] * (M_DIM ** -0.5)).astype(BF16)
    vt_all = v_ref[0].astype(F32).T

    gr = gr_ref[0] + br_ref[...]
    logf_r = _log_sigmoid(gr)

    si = lax.broadcasted_iota(jnp.int32, (L, L), 0)
    ti = lax.broadcasted_iota(jnp.int32, (L, L), 1)
    keep = si <= ti
    tri_t = jnp.where(keep, 1.0, 0.0).astype(BF16)
    ones_rows = jnp.where(lax.broadcasted_iota(jnp.int32, (LANES - M_DIM, L), 0) == 0, 1.0, 0.0)

    for c in range(T // L):
        r0 = c * L
        q_c, k_c = q_all[r0:r0 + L, :], k_all[r0:r0 + L, :]
        cum_r = sum(_dot(p, tri_t) for p in _split3(logf_r[:, r0:r0 + L]))
        ig_r = gr[:, r0:r0 + L]
        col_t = jnp.concatenate([pltpu.roll(ig_r, M_HEADS, 0) - cum_r,
                                 jnp.zeros((LANES - 8, L), F32)], axis=0).T

        staged = []
        for h in range(M_HEADS):
            b_row = cum_r[4 + h:5 + h, :]
            i_row = ig_r[h:h + 1, :]
            m_prev = m_scr[h, 0:1, 0:1]
            q_h = q_c[:, h * M_DIM:(h + 1) * M_DIM]
            k_h = k_c[:, h * M_DIM:(h + 1) * M_DIM]
            vt_ext = jnp.concatenate([vt_all[h * M_DIM:(h + 1) * M_DIM, r0:r0 + L], ones_rows], axis=0)

            b_last = b_row[:, L - 1:L]
            g_row = b_last - b_row + i_row
            m_new = jnp.maximum(b_last + m_prev, jnp.max(g_row, axis=1, keepdims=True))
            w_state = jnp.exp(g_row - m_new)
            decay = jnp.exp(b_last + m_prev - m_new)

            s_t = _nt_dot(k_h, q_h)
            c_old = c_scr[h]
            inter = _nt_dot(c_old.astype(BF16), q_h)
            c_scr[h] = decay * c_old + _dot((vt_ext * w_state).astype(BF16), k_h)
            m_scr[h] = jnp.broadcast_to(m_new, (8, LANES))
            staged.append((s_t, inter, vt_ext, b_row, m_prev))

        heads = []
        for h in range(M_HEADS):
            s_t, inter, vt_ext, b_row, m_prev = staged[h]
            col = col_t[:, 4 + h:5 + h]
            dmat = jnp.where(keep, b_row + col, -jnp.inf)
            m_t = jnp.maximum(b_row + m_prev, jnp.max(dmat, axis=0, keepdims=True))
            sc = (s_t * jnp.exp(dmat - m_t)).astype(BF16)
            a_inter = jnp.exp(b_row + m_prev - m_t)
            num = a_inter * inter + _dot(vt_ext.astype(BF16), sc)
            den = num[M_DIM:M_DIM + 1, :]
            hh = num[:M_DIM, :] * (1.0 / jnp.maximum(jnp.abs(den), jnp.exp(-m_t)))
            inv = lax.rsqrt(jnp.mean(hh * hh, axis=0, keepdims=True) + EPS)
            heads.append(hh * inv * ngt_ref[h * M_DIM:(h + 1) * M_DIM, :])
        hcat = jnp.concatenate(heads, axis=0).T
        gate = _sigmoid(o_ref[0, r0:r0 + L, :].astype(F32))
        out_ref[0, r0:r0 + L, :] = (gate * hcat).astype(BF16)


def _mlstm(qk, mv, mo, gate_r, conv_w, conv_b, bias_r, norm_g, T=1024):
    B, S, _ = qk.shape
    row = lambda width: pl.BlockSpec((1, T, width), lambda b, t: (b, t, 0))
    const = lambda shape: pl.BlockSpec(shape, lambda b, t: (0,) * len(shape))
    return pl.pallas_call(
        functools.partial(_mlstm_kernel, T=T),
        grid=(B, S // T),
        in_specs=[row(512), row(256), row(256),
                  pl.BlockSpec((1, 8, T), lambda b, t: (b, 0, t)),
                  const((CONV_WIDTH, 2 * M_WIDTH)), const((1, 2 * M_WIDTH)),
                  const((8, 1)), const((M_WIDTH, LANES))],
        out_specs=row(M_WIDTH),
        out_shape=jax.ShapeDtypeStruct((B, S, M_WIDTH), BF16),
        scratch_shapes=[pltpu.VMEM((T + 8, 2 * M_WIDTH), F32),
                        pltpu.VMEM((M_HEADS, LANES, M_DIM), F32),
                        pltpu.VMEM((M_HEADS, 8, LANES), F32)],
        compiler_params=pltpu.CompilerParams(
            dimension_semantics=("arbitrary", "arbitrary"), vmem_limit_bytes=VMEM_LIMIT),
        name="mlstm",
    )(qk, mv, mo, gate_r, conv_w, conv_b, bias_r, norm_g)


def _swa_kernel(sink_ref, qt_ref, kc_ref, kp_ref, vtc_ref, vtp_ref, out_ref, s_scr, *, TQ):
    W = WINDOW
    G = S_HEADS // S_KV_HEADS
    tile = pl.program_id(1)
    c = lax.broadcasted_iota(jnp.int32, (2 * W, G * W), 0)
    r = lax.broadcasted_iota(jnp.int32, (2 * W, G * W), 1) % W
    band = (c > r) & (c <= r + W)
    units = [(j, hk) for j in range(TQ // W) for hk in range(S_KV_HEADS)]
    n_slots = s_scr.shape[0]

    def scores(u):
        j, hk = units[u]
        if j == 0:
            kband = jnp.concatenate([kp_ref[0], kc_ref[0, 0:W, :]], axis=0)
        else:
            kband = kc_ref[0, (j - 1) * W:(j + 1) * W, :]
        qs = jnp.concatenate(
            [qt_ref[0, (hk * G + g) * HEAD_DIM:(hk * G + g + 1) * HEAD_DIM, j * W:(j + 1) * W] for g in range(G)],
            axis=1)
        zeros = jnp.zeros_like(qs)
        qpad = jnp.concatenate([qs, zeros] if hk == 0 else [zeros, qs], axis=0)
        s_scr[u % n_slots] = _dot(kband, qpad)

    def softmax_pv(u):
        j, hk = units[u]
        rows = slice(hk * LANES, (hk + 1) * LANES)
        if j == 0:
            vt = jnp.concatenate([vtp_ref[0, rows, :], vtc_ref[0, rows, 0:W]], axis=1)
            valid = band & ((tile > 0) | (c >= W))
        else:
            vt = vtc_ref[0, rows, (j - 1) * W:(j + 1) * W]
            valid = band
        sink = jnp.concatenate([jnp.full((1, W), sink_ref[hk * G + g], F32) for g in range(G)], axis=1)
        s = jnp.where(valid, s_scr[u % n_slots], -jnp.inf)
        mx = jnp.maximum(jnp.max(s, axis=0, keepdims=True), sink)
        e = jnp.exp(s - mx).astype(BF16)
        o = _dot(vt, e)
        denom = o[HEAD_DIM:HEAD_DIM + 1, :] + jnp.exp(sink - mx)
        o = o[:HEAD_DIM, :] * (1.0 / denom)
        return [o[:, g * W:(g + 1) * W] for g in range(G)]

    scores(0)
    scores(1)
    heads = []
    for u in range(len(units)):
        if u + 2 < len(units):
            scores(u + 2)
        heads.extend(softmax_pv(u))
        if len(heads) == S_HEADS:
            j = units[u][0]
            out_ref[0, j * W:(j + 1) * W, :] = jnp.concatenate(heads, axis=0).T.astype(BF16)
            heads = []


def _swa(sqt, sk, svt, sinks, TQ=512):
    B, _, S = sqt.shape
    per = TQ // WINDOW
    prev_blk = lambda t: jnp.maximum(t * per - 1, 0)
    return pl.pallas_call(
        functools.partial(_swa_kernel, TQ=TQ),
        grid=(B, S // TQ),
        in_specs=[pl.BlockSpec(memory_space=pltpu.SMEM),
                  pl.BlockSpec((1, S_WIDTH, TQ), lambda b, t: (b, 0, t)),
                  pl.BlockSpec((1, TQ, S_KV_WIDTH), lambda b, t: (b, t, 0)),
                  pl.BlockSpec((1, WINDOW, S_KV_WIDTH), lambda b, t: (b, prev_blk(t), 0)),
                  pl.BlockSpec((1, S_KV_HEADS * LANES, TQ), lambda b, t: (b, 0, t)),
                  pl.BlockSpec((1, S_KV_HEADS * LANES, WINDOW), lambda b, t: (b, 0, prev_blk(t)))],
        out_specs=pl.BlockSpec((1, TQ, S_WIDTH), lambda b, t: (b, t, 0)),
        out_shape=jax.ShapeDtypeStruct((B, S, S_WIDTH), BF16),
        scratch_shapes=[pltpu.VMEM((3, 2 * WINDOW, (S_HEADS // S_KV_HEADS) * WINDOW), F32)],
        compiler_params=pltpu.CompilerParams(
            dimension_semantics=("arbitrary", "arbitrary"), vmem_limit_bytes=VMEM_LIMIT),
        name="swa",
    )(sinks, sqt, sk, sk, svt, svt)


def _diff_kernel(lamp_ref, subg_ref, q_ref, k_ref, vt_ref, out_ref, s_scr, m_scr, acc_scr, *, TQ, TK, lam_init):
    qi = pl.program_id(2)
    q = q_ref[0]
    lane = lax.broadcasted_iota(jnp.int32, (TQ, LANES), 1)
    zero = jnp.zeros_like(q)
    qg = [jnp.where((lane >= g * D_QK) & (lane < (g + 1) * D_QK), q, zero) for g in range(4)]

    m_scr[...] = jnp.full_like(m_scr, -jnp.inf)
    acc_scr[...] = jnp.zeros_like(acc_scr)

    assert TQ == TK
    below_diag = (lax.broadcasted_iota(jnp.int32, (TK, TQ), 0)
                  <= lax.broadcasted_iota(jnp.int32, (TK, TQ), 1))

    def scores(j, g):
        start = pl.multiple_of(j * TK, TK)
        s_scr[g] = _nt_dot(k_ref[0, pl.ds(start, TK), :], qg[g])

    def softmax_pv(j, g, masked):
        start = pl.multiple_of(j * TK, TK)
        s = s_scr[g]
        if masked:
            s = jnp.where(below_diag, s, -jnp.inf)
        m_old = m_scr[g]
        m_new = jnp.maximum(m_old, jnp.max(s, axis=0, keepdims=True))
        p = jnp.exp2(s - m_new[0:1, :]).astype(BF16)
        alpha = jnp.exp2(m_old[0:1, :] - m_new[0:1, :])
        vt = vt_ref[0, (g // 2) * D_VX:(g // 2 + 1) * D_VX, pl.ds(start, TK)]
        acc_scr[g] = alpha * acc_scr[g] + _dot(vt, p)
        m_scr[g] = m_new

    def pipelined(j, masked, prefetch_next):
        scores(j, 2)
        softmax_pv(j, 0, masked)
        scores(j, 3)
        softmax_pv(j, 1, masked)
        if prefetch_next:
            scores(j + 1, 0)
        softmax_pv(j, 2, masked)
        if prefetch_next:
            scores(j + 1, 1)
        softmax_pv(j, 3, masked)

    n_full = (qi * TQ) // TK
    scores(0, 0)
    scores(0, 1)

    def full_body(j, carry):
        pipelined(j, False, True)
        return carry

    lax.fori_loop(0, n_full, full_body, 0)
    pipelined(n_full, True, False)

    lamp = lamp_ref[...]
    lam = (jnp.exp(jnp.sum(lamp[0:1] * lamp[1:2], axis=-1, keepdims=True))
           - jnp.exp(jnp.sum(lamp[2:3] * lamp[3:4], axis=-1, keepdims=True)) + lam_init)
    outs = []
    g_t = jnp.concatenate([subg_ref[...]] * (TQ // LANES), axis=1) * (1.0 - lam_init)
    for hd in range(2):
        a1 = acc_scr[2 * hd]
        a2 = acc_scr[2 * hd + 1]
        o = a1[:D_V, :] * (1.0 / a1[D_V:D_V + 1, :]) - lam * (a2[:D_V, :] * (1.0 / a2[D_V:D_V + 1, :]))
        outs.append(o * lax.rsqrt(jnp.mean(o * o, axis=0, keepdims=True) + EPS) * g_t)
    out_ref[0] = jnp.concatenate(outs, axis=0).T.astype(BF16)


def _diffattn(dq, dk, dvt, lamp, sub_g, lam_init, TQ=512, TK=512):
    B, S, _ = dq.shape
    return pl.pallas_call(
        functools.partial(_diff_kernel, TQ=TQ, TK=TK, lam_init=lam_init),
        grid=(B, D_HEADS // 2, S // TQ),
        in_specs=[pl.BlockSpec((4, LANES), lambda b, hp, i: (0, 0)),
                  pl.BlockSpec((D_V, LANES), lambda b, hp, i: (0, 0)),
                  pl.BlockSpec((1, TQ, LANES), lambda b, hp, i: (b, i, hp)),
                  pl.BlockSpec((1, S, LANES), lambda b, hp, i: (b, 0, hp)),
                  pl.BlockSpec((1, 2 * D_VX, S), lambda b, hp, i: (b, hp, 0))],
        out_specs=pl.BlockSpec((1, TQ, LANES), lambda b, hp, i: (b, i, hp)),
        out_shape=jax.ShapeDtypeStruct((B, S, D_WIDTH), BF16),
        scratch_shapes=[pltpu.VMEM((4, TK, TQ), F32), pltpu.VMEM((4, 8, TQ), F32),
                        pltpu.VMEM((4, D_VX, TQ), F32)],
        compiler_params=pltpu.CompilerParams(
            dimension_semantics=("arbitrary", "arbitrary", "arbitrary"), vmem_limit_bytes=VMEM_LIMIT),
        name="diffattn",
    )(lamp, sub_g, dq, dk, dvt)


def _out_mlp_kernel(x_ref, om_ref, os_ref, od_ref, wo_ref, wup_ref, wdn_ref,
                    gpm_ref, gpre_ref, gpost_ref, out_ref, *, FC, PARTS):
    half = x_ref.shape[0] // PARTS
    halves = [slice(p * half, (p + 1) * half) for p in range(PARTS)]

    def out_proj(rows):
        return (_dot(om_ref[rows, :], wo_ref[0:M_WIDTH, :])
                + _dot(os_ref[rows, :], wo_ref[M_WIDTH:M_WIDTH + S_WIDTH, :])
                + _dot(od_ref[rows, :], wo_ref[M_WIDTH + S_WIDTH:, :]))

    def mlp(h2):
        acc = jnp.zeros((half, D_MODEL), F32)
        for c in range(D_FF // FC):
            u = jnp.maximum(_dot(h2, wup_ref[:, c * FC:(c + 1) * FC]), 0.0)
            acc = acc + _dot((u * u).astype(BF16), wdn_ref[c * FC:(c + 1) * FC, :])
        return acc

    mixes = [out_proj(rows) for rows in halves]
    for rows, mix in zip(halves, mixes):
        x1 = x_ref[rows, :] + _rms(mix, gpm_ref[...])
        acc = mlp(_rms(x1, gpre_ref[...]).astype(BF16))
        out_ref[rows, :] = x1 + _rms(acc, gpost_ref[...])


def _out_mlp(x2d, om, os_, od, w_out, w_up, w_down, g_post_mix, g_pre_mlp, g_post_mlp, tm=1024, FC=1024, PARTS=4):
    N = x2d.shape[0]
    row = lambda width: pl.BlockSpec((tm, width), lambda i: (i, 0))
    const = lambda shape: pl.BlockSpec(shape, lambda i: (0, 0), pipeline_mode=pl.Buffered(1))
    return pl.pallas_call(
        functools.partial(_out_mlp_kernel, FC=FC, PARTS=PARTS),
        grid=(N // tm,),
        in_specs=[row(D_MODEL), row(M_WIDTH), row(S_WIDTH), row(D_WIDTH),
                  const((D_MODEL, D_MODEL)), const((D_MODEL, D_FF)), const((D_FF, D_MODEL)),
                  const((1, D_MODEL)), const((1, D_MODEL)), const((1, D_MODEL))],
        out_specs=row(D_MODEL),
        out_shape=jax.ShapeDtypeStruct((N, D_MODEL), F32),
        compiler_params=pltpu.CompilerParams(
            dimension_semantics=("arbitrary",), vmem_limit_bytes=VMEM_LIMIT),
        name="out_mlp",
    )(x2d, om, os_, od, w_out, w_up, w_down, g_post_mix, g_pre_mlp, g_post_mlp)


def _rope_tables(S):
    pos = jnp.arange(S, dtype=F32)[:, None]
    lane = np.arange(LANES)

    def table(head_dim):
        half = head_dim // 2
        inv = ROPE_THETA ** (-jnp.arange(half, dtype=F32) * 2.0 / head_dim)
        ang = pos * inv[None, :]
        idx = (lane % head_dim) % half
        sign = np.where((lane % head_dim) < half, -1.0, 1.0).astype(np.float32)
        return jnp.cos(ang)[:, idx], jnp.sin(ang)[:, idx] * sign

    c64, s64 = table(HEAD_DIM)
    c32, s32 = table(D_QK)
    inv = ROPE_THETA ** (-jnp.arange(HEAD_DIM // 2, dtype=F32) * 2.0 / HEAD_DIM)
    ang_t = inv[:, None] * pos.T
    return c64, s64, c32, s32, jnp.cos(ang_t), jnp.sin(ang_t)


def _pack_w_in(w_in):
    sizes = (M_WIDTH, M_WIDTH, M_WIDTH, M_WIDTH, M_HEADS, M_HEADS,
             S_WIDTH, S_KV_WIDTH, S_KV_WIDTH, D_QK_WIDTH, D_QK_WIDTH, D_WIDTH)
    offs = np.concatenate([[0], np.cumsum(sizes)])
    part = [w_in[:, offs[i]:offs[i + 1]] for i in range(len(sizes))]
    mq, mk, mv, mo, mi, mf, sq, sk, sv, dq, dk, dv = part
    zeros = lambda n: jnp.zeros((D_MODEL, n), w_in.dtype)
    dv_blocks = []
    for h in range(D_HEADS):
        dv_blocks += [dv[:, h * D_V:(h + 1) * D_V], zeros(D_VX - D_V)]
    sv_blocks = []
    for h in range(S_KV_HEADS):
        sv_blocks += [sv[:, h * HEAD_DIM:(h + 1) * HEAD_DIM], zeros(LANES - HEAD_DIM)]
    packed = jnp.concatenate([mq, mk, mv, mo, sk, dq, dk], axis=1)
    wg_t = jnp.concatenate([mi, mf], axis=1).T
    wdv_t = jnp.concatenate(dv_blocks, axis=1).T
    wsv_t = jnp.concatenate(sv_blocks, axis=1).T
    return tuple(w.astype(BF16) for w in (packed, wg_t, wdv_t, sq.T, wsv_t))


def kernel(x, w_in, conv_w, conv_b, i_bias, f_bias, m_norm_g, sinks, lam_q1, lam_k1, lam_q2, lam_k2,
           sub_g, w_out, w_up, w_down, g_pre_mix, g_post_mix, g_pre_mlp, g_post_mlp):
    B, S, D = x.shape
    depth = w_in.shape[0]
    tables = _rope_tables(S)
    pad_lanes = lambda v: jnp.pad(v, (0, LANES - v.shape[0]))[None, :]
    lane_bcast = lambda v: jnp.broadcast_to(v[:, None], (v.shape[0], LANES))
    for l in range(depth):
        (qk, mv, mo, gate_r, sqt, sk, svt, dq, dk, dvt) = _in_proj(
            x, g_pre_mix[l][None, :], *_pack_w_in(w_in[l]), tables)

        gate_bias = jnp.concatenate([i_bias[l], f_bias[l]])
        out_m = _mlstm(qk, mv, mo, gate_r, conv_w[l], conv_b[l][None, :],
                       gate_bias[:, None], lane_bcast(m_norm_g[l]))
        out_s = _swa(sqt, sk, svt, sinks[l])
        lamp = jnp.concatenate([pad_lanes(v) for v in (lam_q1[l], lam_k1[l], lam_q2[l], lam_k2[l])], axis=0)
        lam_init = 0.8 - 0.6 * math.exp(-0.3 * l)
        out_d = _diffattn(dq, dk, dvt, lamp, lane_bcast(sub_g[l]), lam_init)

        x = _out_mlp(x.reshape(B * S, D), out_m.reshape(B * S, -1), out_s.reshape(B * S, -1),
                     out_d.reshape(B * S, -1), w_out[l].astype(BF16), w_up[l].astype(BF16),
                     w_down[l].astype(BF16), g_post_mix[l][None, :], g_pre_mlp[l][None, :],
                     g_post_mlp[l][None, :]).reshape(B, S, D)
    return x
```

```python
import functools
import math

import jax
import jax.numpy as jnp
import numpy as np
from jax import lax
from jax.experimental import pallas as pl
from jax.experimental.pallas import tpu as pltpu

D_MODEL = 1024
M_HEADS = 4
M_DIM = 64
CONV_WIDTH = 4
CHUNK = 128
S_HEADS = 8
S_KV_HEADS = 2
HEAD_DIM = 64
WINDOW = 128
D_HEADS = 4
D_QK = 32
D_V = 64
ROPE_THETA = 10000.0
D_FF = 4 * D_MODEL
EPS = 1e-6

M_WIDTH = M_HEADS * M_DIM
S_WIDTH = S_HEADS * HEAD_DIM
S_KV_WIDTH = S_KV_HEADS * HEAD_DIM
D_QK_WIDTH = D_HEADS * 2 * D_QK
D_WIDTH = D_HEADS * D_V
D_VX = D_V + 16

LANES = 128
VMEM_LIMIT = 56 * 1024 * 1024

LOG2E = 1.4426950408889634
KV_UNROLL = 4

BF16 = jnp.bfloat16
F32 = jnp.float32

OFF_QK = 0
OFF_MV = 512
OFF_MO = 768
OFF_SK = 1024
OFF_DQ = 1152
OFF_DK = 1408
PACKED_WIDTH = 1664


def _nt_dot(a, b):
    return lax.dot_general(a, b, (((1,), (1,)), ((), ())), preferred_element_type=F32)


def _dot(a, b):
    return jnp.dot(a, b, preferred_element_type=F32)


def _rms(x, g):
    return x * lax.rsqrt(jnp.mean(x * x, axis=-1, keepdims=True) + EPS) * g


def _sigmoid(x):
    return lax.logistic(x)


def _log_sigmoid(x):
    return jnp.minimum(x, 0.0) - jnp.log(1.0 + jnp.exp(-jnp.abs(x)))


def _split3(x):
    hi = x.astype(BF16)
    r1 = x - hi.astype(F32)
    mid = r1.astype(BF16)
    lo = (r1 - mid.astype(F32)).astype(BF16)
    return hi, mid, lo


def _rope_block(zb, cos, sin_signed, first_half, half):
    fwd = pltpu.roll(zb, LANES - half, 1)
    bwd = pltpu.roll(zb, half, 1)
    return zb * cos + jnp.where(first_half, fwd, bwd) * sin_signed


def _in_proj_kernel(x_ref, g_ref, w_ref, wgt_ref, wdvt_ref, wsqt_ref, wsvt_ref,
                    c64_ref, s64_ref, c32_ref, s32_ref, c64t_ref, s64t_ref,
                    qk_ref, mv_ref, mo_ref, gatet_ref,
                    sqt_ref, sk_ref, svt_ref, dq_ref, dk_ref, dvt_ref, *, PARTS):
    rows_per = x_ref.shape[1] // PARTS
    lane = lax.broadcasted_iota(jnp.int32, (rows_per, LANES), 1)
    first64 = (lane % 64) < 32
    first32 = (lane % 32) < 16
    half = HEAD_DIM // 2
    dq_scale = (D_QK ** -0.5) * LOG2E

    for p in range(PARTS):
        r = slice(p * rows_per, (p + 1) * rows_per)
        h = _rms(x_ref[0, r, :], g_ref[...]).astype(BF16)

        def seg(off, width):
            return _dot(h, w_ref[:, off:off + width])

        qk_ref[0, r, :] = seg(OFF_QK, 512).astype(BF16)
        mv_ref[0, r, :] = seg(OFF_MV, 256).astype(BF16)
        mo_ref[0, r, :] = seg(OFF_MO, 256).astype(BF16)
        gatet_ref[0, :, r] = _nt_dot(wgt_ref[...], h)

        c64, s64 = c64_ref[r, :], s64_ref[r, :]
        c32, s32 = c32_ref[r, :], s32_ref[r, :]

        sqt = _nt_dot(wsqt_ref[...], h)
        ct, st = c64t_ref[:, r], s64t_ref[:, r]
        for hd in range(S_HEADS):
            x1 = sqt[hd * HEAD_DIM:hd * HEAD_DIM + half, :]
            x2 = sqt[hd * HEAD_DIM + half:(hd + 1) * HEAD_DIM, :]
            sqt_ref[0, hd * HEAD_DIM:hd * HEAD_DIM + half, r] = (
                (x1 * ct - x2 * st) * (HEAD_DIM ** -0.5)).astype(BF16)
            sqt_ref[0, hd * HEAD_DIM + half:(hd + 1) * HEAD_DIM, r] = (
                (x2 * ct + x1 * st) * (HEAD_DIM ** -0.5)).astype(BF16)
        sk_ref[0, r, :] = _rope_block(seg(OFF_SK, 128), c64, s64, first64, 32).astype(BF16)
        svt = _nt_dot(wsvt_ref[...], h)
        svt_feat = lax.broadcasted_iota(jnp.int32, svt.shape, 0) % LANES
        svt_ref[0, :, r] = jnp.where(svt_feat == HEAD_DIM, 1.0, svt).astype(BF16)

        dq = seg(OFF_DQ, 256)
        dk = seg(OFF_DK, 256)
        for c in range(2):
            sl = slice(c * LANES, (c + 1) * LANES)
            dq_ref[0, r, sl] = (_rope_block(dq[:, sl], c32, s32, first32, 16) * dq_scale).astype(BF16)
            dk_ref[0, r, sl] = _rope_block(dk[:, sl], c32, s32, first32, 16).astype(BF16)
        dvt = _nt_dot(wdvt_ref[...], h)
        feat = lax.broadcasted_iota(jnp.int32, dvt.shape, 0) % D_VX
        dvt_ref[0, :, r] = jnp.where(feat == D_V, 1.0, dvt).astype(BF16)


def _in_proj(x, g, w_packed, wg_t, wdv_t, wsq_t, wsv_t, tables, tm=1024, PARTS=4):
    B, S, _ = x.shape
    c64, s64, c32, s32, c64t, s64t = tables
    tab_t = pl.BlockSpec((HEAD_DIM // 2, tm), lambda si, b: (0, si))
    row = lambda width: pl.BlockSpec((1, tm, width), lambda si, b: (b, si, 0))
    const = lambda shape: pl.BlockSpec(shape, lambda si, b: (0,) * len(shape))
    tab = pl.BlockSpec((tm, LANES), lambda si, b: (si, 0))
    out_shape = [
        jax.ShapeDtypeStruct((B, S, 512), BF16),
        jax.ShapeDtypeStruct((B, S, 256), BF16),
        jax.ShapeDtypeStruct((B, S, 256), BF16),
        jax.ShapeDtypeStruct((B, 8, S), F32),
        jax.ShapeDtypeStruct((B, 512, S), BF16),
        jax.ShapeDtypeStruct((B, S, 128), BF16),
        jax.ShapeDtypeStruct((B, 256, S), BF16),
        jax.ShapeDtypeStruct((B, S, 256), BF16),
        jax.ShapeDtypeStruct((B, S, 256), BF16),
        jax.ShapeDtypeStruct((B, D_HEADS * D_VX, S), BF16),
    ]
    out_specs = [row(512), row(256), row(256),
                 pl.BlockSpec((1, 8, tm), lambda si, b: (b, 0, si)),
                 pl.BlockSpec((1, 512, tm), lambda si, b: (b, 0, si)), row(128),
                 pl.BlockSpec((1, 256, tm), lambda si, b: (b, 0, si)), row(256), row(256),
                 pl.BlockSpec((1, D_HEADS * D_VX, tm), lambda si, b: (b, 0, si))]
    return pl.pallas_call(
        functools.partial(_in_proj_kernel, PARTS=PARTS),
        grid=(S // tm, B),
        in_specs=[row(D_MODEL), const((1, D_MODEL)), const((D_MODEL, PACKED_WIDTH)),
                  const((8, D_MODEL)), const((D_HEADS * D_VX, D_MODEL)), const((512, D_MODEL)), const((256, D_MODEL)),
                  tab, tab, tab, tab, tab_t, tab_t],
        out_specs=out_specs,
        out_shape=out_shape,
        compiler_params=pltpu.CompilerParams(
            dimension_semantics=("arbitrary", "arbitrary"), vmem_limit_bytes=VMEM_LIMIT),
        name="in_proj",
    )(x, g, w_packed, wg_t, wdv_t, wsq_t, wsv_t, c64, s64, c32, s32, c64t, s64t)


def _mlstm_kernel(qk_ref, v_ref, o_ref, gr_ref, cw_ref, cb_ref, br_ref, ngt_ref,
                  out_ref, xbuf, c_scr, m_scr, *, T):
    L = CHUNK

    @pl.when(pl.program_id(1) == 0)
    def _():
        xbuf[0:8, :] = jnp.zeros((8, 2 * M_WIDTH), F32)
        c_scr[...] = jnp.zeros_like(c_scr)
        m_scr[...] = jnp.zeros_like(m_scr)

    xbuf[8:8 + T, :] = qk_ref[0].astype(F32)
    y = cb_ref[...] + sum(cw_ref[j:j + 1, :] * xbuf[5 + j:5 + j + T, :] for j in range(CONV_WIDTH))
    xbuf[0:8, :] = xbuf[T:T + 8, :]
    act = y * _sigmoid(y)
    q_all = act[:, :M_WIDTH].astype(BF16)
    k_all = (act[:, M_WIDTH:] * (M_DIM ** -0.5)).astype(BF16)
    vt_all = v_ref[0].astype(F32).T

    gr = gr_ref[0] + br_ref[...]
    logf_r = _log_sigmoid(gr)

    si = lax.broadcasted_iota(jnp.int32, (L, L), 0)
    ti = lax.broadcasted_iota(jnp.int32, (L, L), 1)
    keep = si <= ti
    tri_t = jnp.where(keep, 1.0, 0.0).astype(BF16)
    ones_rows = jnp.where(lax.broadcasted_iota(jnp.int32, (LANES - M_DIM, L), 0) == 0, 1.0, 0.0)

    for c in range(T // L):
        r0 = c * L
        q_c, k_c = q_all[r0:r0 + L, :], k_all[r0:r0 + L, :]
        cum_r = sum(_dot(p, tri_t) for p in _split3(logf_r[:, r0:r0 + L]))
        ig_r = gr[:, r0:r0 + L]
        col_t = jnp.concatenate([pltpu.roll(ig_r, M_HEADS, 0) - cum_r,
                                 jnp.zeros((LANES - 8, L), F32)], axis=0).T

        staged = []
        for h in range(M_HEADS):
            b_row = cum_r[4 + h:5 + h, :]
            i_row = ig_r[h:h + 1, :]
            m_prev = m_scr[h, 0:1, 0:1]
            q_h = q_c[:, h * M_DIM:(h + 1) * M_DIM]
            k_h = k_c[:, h * M_DIM:(h + 1) * M_DIM]
            vt_ext = jnp.concatenate([vt_all[h * M_DIM:(h + 1) * M_DIM, r0:r0 + L], ones_rows], axis=0)

            b_last = b_row[:, L - 1:L]
            g_row = b_last - b_row + i_row
            m_new = jnp.maximum(b_last + m_prev, jnp.max(g_row, axis=1, keepdims=True))
            w_state = jnp.exp(g_row - m_new)
            decay = jnp.exp(b_last + m_prev - m_new)

            s_t = _nt_dot(k_h, q_h)
            c_old = c_scr[h]
            inter = _nt_dot(c_old.astype(BF16), q_h)
            c_scr[h] = decay * c_old + _dot((vt_ext * w_state).astype(BF16), k_h)
            m_scr[h] = jnp.broadcast_to(m_new, (8, LANES))
            staged.append((s_t, inter, vt_ext, b_row, m_prev))

        heads = []
        for h in range(M_HEADS):
            s_t, inter, vt_ext, b_row, m_prev = staged[h]
            col = col_t[:, 4 + h:5 + h]
            dmat = jnp.where(keep, b_row + col, -jnp.inf)
            m_t = jnp.maximum(b_row + m_prev, jnp.max(dmat, axis=0, keepdims=True))
            sc = (s_t * jnp.exp(dmat - m_t)).astype(BF16)
            a_inter = jnp.exp(b_row + m_prev - m_t)
            num = a_inter * inter + _dot(vt_ext.astype(BF16), sc)
            den = num[M_DIM:M_DIM + 1, :]
            hh = num[:M_DIM, :] * (1.0 / jnp.maximum(jnp.abs(den), jnp.exp(-m_t)))
            inv = lax.rsqrt(jnp.mean(hh * hh, axis=0, keepdims=True) + EPS)
            heads.append(hh * inv * ngt_ref[h * M_DIM:(h + 1) * M_DIM, :])
        hcat = jnp.concatenate(heads, axis=0).T
        gate = _sigmoid(o_ref[0, r0:r0 + L, :].astype(F32))
        out_ref[0, r0:r0 + L, :] = (gate * hcat).astype(BF16)


def _mlstm(qk, mv, mo, gate_r, conv_w, conv_b, bias_r, norm_g, T=1024):
    B, S, _ = qk.shape
    row = lambda width: pl.BlockSpec((1, T, width), lambda b, t: (b, t, 0))
    const = lambda shape: pl.BlockSpec(shape, lambda b, t: (0,) * len(shape))
    return pl.pallas_call(
        functools.partial(_mlstm_kernel, T=T),
        grid=(B, S // T),
        in_specs=[row(512), row(256), row(256),
                  pl.BlockSpec((1, 8, T), lambda b, t: (b, 0, t)),
                  const((CONV_WIDTH, 2 * M_WIDTH)), const((1, 2 * M_WIDTH)),
                  const((8, 1)), const((M_WIDTH, LANES))],
        out_specs=row(M_WIDTH),
        out_shape=jax.ShapeDtypeStruct((B, S, M_WIDTH), BF16),
        scratch_shapes=[pltpu.VMEM((T + 8, 2 * M_WIDTH), F32),
                        pltpu.VMEM((M_HEADS, LANES, M_DIM), F32),
                        pltpu.VMEM((M_HEADS, 8, LANES), F32)],
        compiler_params=pltpu.CompilerParams(
            dimension_semantics=("arbitrary", "arbitrary"), vmem_limit_bytes=VMEM_LIMIT),
        name="mlstm",
    )(qk, mv, mo, gate_r, conv_w, conv_b, bias_r, norm_g)


def _swa_kernel(sink_ref, qt_ref, kc_ref, kp_ref, vtc_ref, vtp_ref, out_ref, s_scr, *, TQ):
    W = WINDOW
    G = S_HEADS // S_KV_HEADS
    tile = pl.program_id(1)
    c = lax.broadcasted_iota(jnp.int32, (2 * W, G * W), 0)
    r = lax.broadcasted_iota(jnp.int32, (2 * W, G * W), 1) % W
    band = (c > r) & (c <= r + W)
    units = [(j, hk) for j in range(TQ // W) for hk in range(S_KV_HEADS)]
    n_slots = s_scr.shape[0]

    def scores(u):
        j, hk = units[u]
        if j == 0:
            kband = jnp.concatenate([kp_ref[0], kc_ref[0, 0:W, :]], axis=0)
        else:
            kband = kc_ref[0, (j - 1) * W:(j + 1) * W, :]
        qs = jnp.concatenate(
            [qt_ref[0, (hk * G + g) * HEAD_DIM:(hk * G + g + 1) * HEAD_DIM, j * W:(j + 1) * W] for g in range(G)],
            axis=1)
        zeros = jnp.zeros_like(qs)
        qpad = jnp.concatenate([qs, zeros] if hk == 0 else [zeros, qs], axis=0)
        s_scr[u % n_slots] = _dot(kband, qpad)

    def softmax_pv(u):
        j, hk = units[u]
        rows = slice(hk * LANES, (hk + 1) * LANES)
        if j == 0:
            vt = jnp.concatenate([vtp_ref[0, rows, :], vtc_ref[0, rows, 0:W]], axis=1)
            valid = band & ((tile > 0) | (c >= W))
        else:
            vt = vtc_ref[0, rows, (j - 1) * W:(j + 1) * W]
            valid = band
        sink = jnp.concatenate([jnp.full((1, W), sink_ref[hk * G + g], F32) for g in range(G)], axis=1)
        s = jnp.where(valid, s_scr[u % n_slots], -jnp.inf)
        mx = jnp.maximum(jnp.max(s, axis=0, keepdims=True), sink)
        e = jnp.exp(s - mx).astype(BF16)
        o = _dot(vt, e)
        denom = o[HEAD_DIM:HEAD_DIM + 1, :] + jnp.exp(sink - mx)
        o = o[:HEAD_DIM, :] * (1.0 / denom)
        return [o[:, g * W:(g + 1) * W] for g in range(G)]

    scores(0)
    scores(1)
    heads = []
    for u in range(len(units)):
        if u + 2 < len(units):
            scores(u + 2)
        heads.extend(softmax_pv(u))
        if len(heads) == S_HEADS:
            j = units[u][0]
            out_ref[0, j * W:(j + 1) * W, :] = jnp.concatenate(heads, axis=0).T.astype(BF16)
            heads = []


def _swa(sqt, sk, svt, sinks, TQ=512):
    B, _, S = sqt.shape
    per = TQ // WINDOW
    prev_blk = lambda t: jnp.maximum(t * per - 1, 0)
    return pl.pallas_call(
        functools.partial(_swa_kernel, TQ=TQ),
        grid=(B, S // TQ),
        in_specs=[pl.BlockSpec(memory_space=pltpu.SMEM),
                  pl.BlockSpec((1, S_WIDTH, TQ), lambda b, t: (b, 0, t)),
                  pl.BlockSpec((1, TQ, S_KV_WIDTH), lambda b, t: (b, t, 0)),
                  pl.BlockSpec((1, WINDOW, S_KV_WIDTH), lambda b, t: (b, prev_blk(t), 0)),
                  pl.BlockSpec((1, S_KV_HEADS * LANES, TQ), lambda b, t: (b, 0, t)),
                  pl.BlockSpec((1, S_KV_HEADS * LANES, WINDOW), lambda b, t: (b, 0, prev_blk(t)))],
        out_specs=pl.BlockSpec((1, TQ, S_WIDTH), lambda b, t: (b, t, 0)),
        out_shape=jax.ShapeDtypeStruct((B, S, S_WIDTH), BF16),
        scratch_shapes=[pltpu.VMEM((3, 2 * WINDOW, (S_HEADS // S_KV_HEADS) * WINDOW), F32)],
        compiler_params=pltpu.CompilerParams(
            dimension_semantics=("arbitrary", "arbitrary"), vmem_limit_bytes=VMEM_LIMIT),
        name="swa",
    )(sinks, sqt, sk, sk, svt, svt)


def _diff_kernel(lamp_ref, subg_ref, q_ref, k_ref, vt_ref, out_ref, s_scr, m_scr, acc_scr, *, TQ, TK, lam_init):
    qi = pl.program_id(2)
    q = q_ref[0]
    lane = lax.broadcasted_iota(jnp.int32, (TQ, LANES), 1)
    zero = jnp.zeros_like(q)
    qg = [jnp.where((lane >= g * D_QK) & (lane < (g + 1) * D_QK), q, zero) for g in range(4)]

    m_scr[...] = jnp.full_like(m_scr, -jnp.inf)
    acc_scr[...] = jnp.zeros_like(acc_scr)

    assert TQ == TK
    below_diag = (lax.broadcasted_iota(jnp.int32, (TK, TQ), 0)
                  <= lax.broadcasted_iota(jnp.int32, (TK, TQ), 1))

    def scores(j, g):
        start = pl.multiple_of(j * TK, TK)
        s_scr[g] = _nt_dot(k_ref[0, pl.ds(start, TK), :], qg[g])

    def softmax_pv(j, g, masked):
        start = pl.multiple_of(j * TK, TK)
        s = s_scr[g]
        if masked:
            s = jnp.where(below_diag, s, -jnp.inf)
        m_old = m_scr[g]
        m_new = jnp.maximum(m_old, jnp.max(s, axis=0, keepdims=True))
        p = jnp.exp2(s - m_new[0:1, :]).astype(BF16)
        alpha = jnp.exp2(m_old[0:1, :] - m_new[0:1, :])
        vt = vt_ref[0, (g // 2) * D_VX:(g // 2 + 1) * D_VX, pl.ds(start, TK)]
        acc_scr[g] = alpha * acc_scr[g] + _dot(vt, p)
        m_scr[g] = m_new

    def pipelined(j, masked, prefetch_next):
        scores(j, 2)
        softmax_pv(j, 0, masked)
        scores(j, 3)
        softmax_pv(j, 1, masked)
        if prefetch_next:
            scores(j + 1, 0)
        softmax_pv(j, 2, masked)
        if prefetch_next:
            scores(j + 1, 1)
        softmax_pv(j, 3, masked)

    n_full = (qi * TQ) // TK
    scores(0, 0)
    scores(0, 1)

    def unrolled_body(jj, carry):
        for u in range(KV_UNROLL):
            pipelined(KV_UNROLL * jj + u, False, True)
        return carry

    lax.fori_loop(0, n_full // KV_UNROLL, unrolled_body, 0)
    rem = n_full % KV_UNROLL
    for u in range(KV_UNROLL - 1):
        @pl.when(rem > u)
        def _():
            pipelined(n_full - rem + u, False, True)

    pipelined(n_full, True, False)

    lamp = lamp_ref[...]
    lam = (jnp.exp(jnp.sum(lamp[0:1] * lamp[1:2], axis=-1, keepdims=True))
           - jnp.exp(jnp.sum(lamp[2:3] * lamp[3:4], axis=-1, keepdims=True)) + lam_init)
    outs = []
    g_t = jnp.concatenate([subg_ref[...]] * (TQ // LANES), axis=1) * (1.0 - lam_init)
    for hd in range(2):
        a1 = acc_scr[2 * hd]
        a2 = acc_scr[2 * hd + 1]
        o = a1[:D_V, :] * (1.0 / a1[D_V:D_V + 1, :]) - lam * (a2[:D_V, :] * (1.0 / a2[D_V:D_V + 1, :]))
        outs.append(o * lax.rsqrt(jnp.mean(o * o, axis=0, keepdims=True) + EPS) * g_t)
    out_ref[0] = jnp.concatenate(outs, axis=0).T.astype(BF16)


def _diffattn(dq, dk, dvt, lamp, sub_g, lam_init, TQ=512, TK=512):
    B, S, _ = dq.shape
    return pl.pallas_call(
        functools.partial(_diff_kernel, TQ=TQ, TK=TK, lam_init=lam_init),
        grid=(B, D_HEADS // 2, S // TQ),
        in_specs=[pl.BlockSpec((4, LANES), lambda b, hp, i: (0, 0)),
                  pl.BlockSpec((D_V, LANES), lambda b, hp, i: (0, 0)),
                  pl.BlockSpec((1, TQ, LANES), lambda b, hp, i: (b, i, hp)),
                  pl.BlockSpec((1, S, LANES), lambda b, hp, i: (b, 0, hp)),
                  pl.BlockSpec((1, 2 * D_VX, S), lambda b, hp, i: (b, hp, 0))],
        out_specs=pl.BlockSpec((1, TQ, LANES), lambda b, hp, i: (b, i, hp)),
        out_shape=jax.ShapeDtypeStruct((B, S, D_WIDTH), BF16),
        scratch_shapes=[pltpu.VMEM((4, TK, TQ), F32), pltpu.VMEM((4, 8, TQ), F32),
                        pltpu.VMEM((4, D_VX, TQ), F32)],
        compiler_params=pltpu.CompilerParams(
            dimension_semantics=("arbitrary", "arbitrary", "arbitrary"), vmem_limit_bytes=VMEM_LIMIT),
        name="diffattn",
    )(lamp, sub_g, dq, dk, dvt)


def _out_mlp_kernel(x_ref, om_ref, os_ref, od_ref, wo_ref, wup_ref, wdn_ref,
                    gpm_ref, gpre_ref, gpost_ref, out_ref, *, FC, PARTS):
    half = x_ref.shape[0] // PARTS
    halves = [slice(p * half, (p + 1) * half) for p in range(PARTS)]

    def out_proj(rows):
        return (_dot(om_ref[rows, :], wo_ref[0:M_WIDTH, :])
                + _dot(os_ref[rows, :], wo_ref[M_WIDTH:M_WIDTH + S_WIDTH, :])
                + _dot(od_ref[rows, :], wo_ref[M_WIDTH + S_WIDTH:, :]))

    def mlp(h2):
        acc = jnp.zeros((half, D_MODEL), F32)
        for c in range(D_FF // FC):
            u = jnp.maximum(_dot(h2, wup_ref[:, c * FC:(c + 1) * FC]), 0.0)
            acc = acc + _dot((u * u).astype(BF16), wdn_ref[c * FC:(c + 1) * FC, :])
        return acc

    mixes = [out_proj(rows) for rows in halves]
    for rows, mix in zip(halves, mixes):
        x1 = x_ref[rows, :] + _rms(mix, gpm_ref[...])
        acc = mlp(_rms(x1, gpre_ref[...]).astype(BF16))
        out_ref[rows, :] = x1 + _rms(acc, gpost_ref[...])


def _out_mlp(x2d, om, os_, od, w_out, w_up, w_down, g_post_mix, g_pre_mlp, g_post_mlp, tm=1024, FC=1024, PARTS=4):
    N = x2d.shape[0]
    row = lambda width: pl.BlockSpec((tm, width), lambda i: (i, 0))
    const = lambda shape: pl.BlockSpec(shape, lambda i: (0, 0), pipeline_mode=pl.Buffered(1))
    return pl.pallas_call(
        functools.partial(_out_mlp_kernel, FC=FC, PARTS=PARTS),
        grid=(N // tm,),
        in_specs=[row(D_MODEL), row(M_WIDTH), row(S_WIDTH), row(D_WIDTH),
                  const((D_MODEL, D_MODEL)), const((D_MODEL, D_FF)), const((D_FF, D_MODEL)),
                  const((1, D_MODEL)), const((1, D_MODEL)), const((1, D_MODEL))],
        out_specs=row(D_MODEL),
        out_shape=jax.ShapeDtypeStruct((N, D_MODEL), F32),
        compiler_params=pltpu.CompilerParams(
            dimension_semantics=("arbitrary",), vmem_limit_bytes=VMEM_LIMIT),
        name="out_mlp",
    )(x2d, om, os_, od, w_out, w_up, w_down, g_post_mix, g_pre_mlp, g_post_mlp)


def _rope_tables(S):
    pos = jnp.arange(S, dtype=F32)[:, None]
    lane = np.arange(LANES)

    def table(head_dim):
        half = head_dim // 2
        inv = ROPE_THETA ** (-jnp.arange(half, dtype=F32) * 2.0 / head_dim)
        ang = pos * inv[None, :]
        idx = (lane % head_dim) % half
        sign = np.where((lane % head_dim) < half, -1.0, 1.0).astype(np.float32)
        return jnp.cos(ang)[:, idx], jnp.sin(ang)[:, idx] * sign

    c64, s64 = table(HEAD_DIM)
    c32, s32 = table(D_QK)
    inv = ROPE_THETA ** (-jnp.arange(HEAD_DIM // 2, dtype=F32) * 2.0 / HEAD_DIM)
    ang_t = inv[:, None] * pos.T
    return c64, s64, c32, s32, jnp.cos(ang_t), jnp.sin(ang_t)


def _pack_w_in(w_in):
    sizes = (M_WIDTH, M_WIDTH, M_WIDTH, M_WIDTH, M_HEADS, M_HEADS,
             S_WIDTH, S_KV_WIDTH, S_KV_WIDTH, D_QK_WIDTH, D_QK_WIDTH, D_WIDTH)
    offs = np.concatenate([[0], np.cumsum(sizes)])
    part = [w_in[:, offs[i]:offs[i + 1]] for i in range(len(sizes))]
    mq, mk, mv, mo, mi, mf, sq, sk, sv, dq, dk, dv = part
    zeros = lambda n: jnp.zeros((D_MODEL, n), w_in.dtype)
    dv_blocks = []
    for h in range(D_HEADS):
        dv_blocks += [dv[:, h * D_V:(h + 1) * D_V], zeros(D_VX - D_V)]
    sv_blocks = []
    for h in range(S_KV_HEADS):
        sv_blocks += [sv[:, h * HEAD_DIM:(h + 1) * HEAD_DIM], zeros(LANES - HEAD_DIM)]
    packed = jnp.concatenate([mq, mk, mv, mo, sk, dq, dk], axis=1)
    wg_t = jnp.concatenate([mi, mf], axis=1).T
    wdv_t = jnp.concatenate(dv_blocks, axis=1).T
    wsv_t = jnp.concatenate(sv_blocks, axis=1).T
    return tuple(w.astype(BF16) for w in (packed, wg_t, wdv_t, sq.T, wsv_t))


def kernel(x, w_in, conv_w, conv_b, i_bias, f_bias, m_norm_g, sinks, lam_q1, lam_k1, lam_q2, lam_k2,
           sub_g, w_out, w_up, w_down, g_pre_mix, g_post_mix, g_pre_mlp, g_post_mlp):
    B, S, D = x.shape
    depth = w_in.shape[0]
    tables = _rope_tables(S)
    pad_lanes = lambda v: jnp.pad(v, (0, LANES - v.shape[0]))[None, :]
    lane_bcast = lambda v: jnp.broadcast_to(v[:, None], (v.shape[0], LANES))
    for l in range(depth):
        (qk, mv, mo, gate_r, sqt, sk, svt, dq, dk, dvt) = _in_proj(
            x, g_pre_mix[l][None, :], *_pack_w_in(w_in[l]), tables)

        gate_bias = jnp.concatenate([i_bias[l], f_bias[l]])
        out_m = _mlstm(qk, mv, mo, gate_r, conv_w[l], conv_b[l][None, :],
                       gate_bias[:, None], lane_bcast(m_norm_g[l]))
        out_s = _swa(sqt, sk, svt, sinks[l])
        lamp = jnp.concatenate([pad_lanes(v) for v in (lam_q1[l], lam_k1[l], lam_q2[l], lam_k2[l])], axis=0)
        lam_init = 0.8 - 0.6 * math.exp(-0.3 * l)
        out_d = _diffattn(dq, dk, dvt, lamp, lane_bcast(sub_g[l]), lam_init)

        x = _out_mlp(x.reshape(B * S, D), out_m.reshape(B * S, -1), out_s.reshape(B * S, -1),
                     out_d.reshape(B * S, -1), w_out[l].astype(BF16), w_up[l].astype(BF16),
                     w_down[l].astype(BF16), g_post_mix[l][None, :], g_pre_mlp[l][None, :],
                     g_post_mlp[l][None, :]).reshape(B, S, D)
    return x
```

```python
import functools
import math

import jax
import jax.numpy as jnp
import numpy as np
from jax import lax
from jax.experimental import pallas as pl
from jax.experimental.pallas import tpu as pltpu

D_MODEL = 1024
M_HEADS = 4
M_DIM = 64
CONV_WIDTH = 4
CHUNK = 128
S_HEADS = 8
S_KV_HEADS = 2
HEAD_DIM = 64
WINDOW = 128
D_HEADS = 4
D_QK = 32
D_V = 64
ROPE_THETA = 10000.0
D_FF = 4 * D_MODEL
EPS = 1e-6

M_WIDTH = M_HEADS * M_DIM
S_WIDTH = S_HEADS * HEAD_DIM
S_KV_WIDTH = S_KV_HEADS * HEAD_DIM
D_QK_WIDTH = D_HEADS * 2 * D_QK
D_WIDTH = D_HEADS * D_V
D_VX = D_V + 16

LANES = 128
VMEM_LIMIT = 56 * 1024 * 1024

LOG2E = 1.4426950408889634
KV_UNROLL = 4

BF16 = jnp.bfloat16
F32 = jnp.float32

OFF_QK = 0
OFF_MV = 512
OFF_MO = 768
OFF_SK = 1024
OFF_DQ = 1152
OFF_DK = 1408
PACKED_WIDTH = 1664


def _nt_dot(a, b):
    return lax.dot_general(a, b, (((1,), (1,)), ((), ())), preferred_element_type=F32)


def _dot(a, b):
    return jnp.dot(a, b, preferred_element_type=F32)


def _rms(x, g):
    return x * lax.rsqrt(jnp.mean(x * x, axis=-1, keepdims=True) + EPS) * g


def _sigmoid(x):
    return lax.logistic(x)


def _log_sigmoid(x):
    return jnp.minimum(x, 0.0) - jnp.log(1.0 + jnp.exp(-jnp.abs(x)))


def _split3(x):
    hi = x.astype(BF16)
    r1 = x - hi.astype(F32)
    mid = r1.astype(BF16)
    lo = (r1 - mid.astype(F32)).astype(BF16)
    return hi, mid, lo


def _rope_block(zb, cos, sin_signed, first_half, half):
    fwd = pltpu.roll(zb, LANES - half, 1)
    bwd = pltpu.roll(zb, half, 1)
    return zb * cos + jnp.where(first_half, fwd, bwd) * sin_signed


def _in_proj_kernel(x_ref, g_ref, w_ref, wgt_ref, wdvt_ref, wsqt_ref, wsvt_ref,
                    c64_ref, s64_ref, c32_ref, s32_ref, c64t_ref, s64t_ref,
                    qk_ref, mv_ref, mo_ref, gatet_ref,
                    sqt_ref, sk_ref, svt_ref, dq_ref, dk_ref, dvt_ref, *, PARTS):
    rows_per = x_ref.shape[1] // PARTS
    lane = lax.broadcasted_iota(jnp.int32, (rows_per, LANES), 1)
    first64 = (lane % 64) < 32
    first32 = (lane % 32) < 16
    half = HEAD_DIM // 2
    dq_scale = (D_QK ** -0.5) * LOG2E

    for p in range(PARTS):
        r = slice(p * rows_per, (p + 1) * rows_per)
        h = _rms(x_ref[0, r, :], g_ref[...]).astype(BF16)

        def seg(off, width):
            return _dot(h, w_ref[:, off:off + width])

        qk_ref[0, r, :] = seg(OFF_QK, 512).astype(BF16)
        mv_ref[0, r, :] = seg(OFF_MV, 256).astype(BF16)
        mo_ref[0, r, :] = seg(OFF_MO, 256).astype(BF16)
        gatet_ref[0, :, r] = _nt_dot(wgt_ref[...], h)

        c64, s64 = c64_ref[r, :], s64_ref[r, :]
        c32, s32 = c32_ref[r, :], s32_ref[r, :]

        sqt = _nt_dot(wsqt_ref[...], h)
        ct, st = c64t_ref[:, r], s64t_ref[:, r]
        for hd in range(S_HEADS):
            x1 = sqt[hd * HEAD_DIM:hd * HEAD_DIM + half, :]
            x2 = sqt[hd * HEAD_DIM + half:(hd + 1) * HEAD_DIM, :]
            sqt_ref[0, hd * HEAD_DIM:hd * HEAD_DIM + half, r] = (
                (x1 * ct - x2 * st) * (HEAD_DIM ** -0.5)).astype(BF16)
            sqt_ref[0, hd * HEAD_DIM + half:(hd + 1) * HEAD_DIM, r] = (
                (x2 * ct + x1 * st) * (HEAD_DIM ** -0.5)).astype(BF16)
        sk_ref[0, r, :] = _rope_block(seg(OFF_SK, 128), c64, s64, first64, 32).astype(BF16)
        svt = _nt_dot(wsvt_ref[...], h)
        svt_feat = lax.broadcasted_iota(jnp.int32, svt.shape, 0) % LANES
        svt_ref[0, :, r] = jnp.where(svt_feat == HEAD_DIM, 1.0, svt).astype(BF16)

        dq = seg(OFF_DQ, 256)
        dk = seg(OFF_DK, 256)
        for c in range(2):
            sl = slice(c * LANES, (c + 1) * LANES)
            dq_ref[0, r, sl] = (_rope_block(dq[:, sl], c32, s32, first32, 16) * dq_scale).astype(BF16)
            dk_ref[0, r, sl] = _rope_block(dk[:, sl], c32, s32, first32, 16).astype(BF16)
        dvt = _nt_dot(wdvt_ref[...], h)
        feat = lax.broadcasted_iota(jnp.int32, dvt.shape, 0) % D_VX
        dvt_ref[0, :, r] = jnp.where(feat == D_V, 1.0, dvt).astype(BF16)


def _in_proj(x, g, w_packed, wg_t, wdv_t, wsq_t, wsv_t, tables, tm=1024, PARTS=4):
    B, S, _ = x.shape
    c64, s64, c32, s32, c64t, s64t = tables
    tab_t = pl.BlockSpec((HEAD_DIM // 2, tm), lambda si, b: (0, si))
    row = lambda width: pl.BlockSpec((1, tm, width), lambda si, b: (b, si, 0))
    const = lambda shape: pl.BlockSpec(shape, lambda si, b: (0,) * len(shape))
    tab = pl.BlockSpec((tm, LANES), lambda si, b: (si, 0))
    out_shape = [
        jax.ShapeDtypeStruct((B, S, 512), BF16),
        jax.ShapeDtypeStruct((B, S, 256), BF16),
        jax.ShapeDtypeStruct((B, S, 256), BF16),
        jax.ShapeDtypeStruct((B, 8, S), F32),
        jax.ShapeDtypeStruct((B, 512, S), BF16),
        jax.ShapeDtypeStruct((B, S, 128), BF16),
        jax.ShapeDtypeStruct((B, 256, S), BF16),
        jax.ShapeDtypeStruct((B, S, 256), BF16),
        jax.ShapeDtypeStruct((B, S, 256), BF16),
        jax.ShapeDtypeStruct((B, D_HEADS * D_VX, S), BF16),
    ]
    out_specs = [row(512), row(256), row(256),
                 pl.BlockSpec((1, 8, tm), lambda si, b: (b, 0, si)),
                 pl.BlockSpec((1, 512, tm), lambda si, b: (b, 0, si)), row(128),
                 pl.BlockSpec((1, 256, tm), lambda si, b: (b, 0, si)), row(256), row(256),
                 pl.BlockSpec((1, D_HEADS * D_VX, tm), lambda si, b: (b, 0, si))]
    return pl.pallas_call(
        functools.partial(_in_proj_kernel, PARTS=PARTS),
        grid=(S // tm, B),
        in_specs=[row(D_MODEL), const((1, D_MODEL)), const((D_MODEL, PACKED_WIDTH)),
                  const((8, D_MODEL)), const((D_HEADS * D_VX, D_MODEL)), const((512, D_MODEL)), const((256, D_MODEL)),
                  tab, tab, tab, tab, tab_t, tab_t],
        out_specs=out_specs,
        out_shape=out_shape,
        compiler_params=pltpu.CompilerParams(
            dimension_semantics=("arbitrary", "arbitrary"), vmem_limit_bytes=VMEM_LIMIT),
        name="in_proj",
    )(x, g, w_packed, wg_t, wdv_t, wsq_t, wsv_t, c64, s64, c32, s32, c64t, s64t)


def _mlstm_kernel(qk_ref, v_ref, o_ref, gr_ref, cw_ref, cb_ref, br_ref, ngt_ref,
                  out_ref, xbuf, c_scr, m_scr, *, T):
    L = CHUNK

    @pl.when(pl.program_id(1) == 0)
    def _():
        xbuf[0:8, :] = jnp.zeros((8, 2 * M_WIDTH), F32)
        c_scr[...] = jnp.zeros_like(c_scr)
        m_scr[...] = jnp.zeros_like(m_scr)

    xbuf[8:8 + T, :] = qk_ref[0].astype(F32)
    y = cb_ref[...] + sum(cw_ref[j:j + 1, :] * xbuf[5 + j:5 + j + T, :] for j in range(CONV_WIDTH))
    xbuf[0:8, :] = xbuf[T:T + 8, :]
    act = y * _sigmoid(y)
    q_all = act[:, :M_WIDTH].astype(BF16)
    k_all = (act[:, M_WIDTH:] * (M_DIM ** -0.5)).astype(BF16)
    vt_all = v_ref[0].astype(F32).T

    gr = gr_ref[0] + br_ref[...]
    logf_r = _log_sigmoid(gr)

    si = lax.broadcasted_iota(jnp.int32, (L, L), 0)
    ti = lax.broadcasted_iota(jnp.int32, (L, L), 1)
    keep = si <= ti
    tri_t = jnp.where(keep, 1.0, 0.0).astype(BF16)
    ones_rows = jnp.where(lax.broadcasted_iota(jnp.int32, (LANES - M_DIM, L), 0) == 0, 1.0, 0.0)

    for c in range(T // L):
        r0 = c * L
        q_c, k_c = q_all[r0:r0 + L, :], k_all[r0:r0 + L, :]
        cum_r = sum(_dot(p, tri_t) for p in _split3(logf_r[:, r0:r0 + L]))
        ig_r = gr[:, r0:r0 + L]
        col_t = jnp.concatenate([pltpu.roll(ig_r, M_HEADS, 0) - cum_r,
                                 jnp.zeros((LANES - 8, L), F32)], axis=0).T

        staged = []
        for h in range(M_HEADS):
            b_row = cum_r[4 + h:5 + h, :]
            i_row = ig_r[h:h + 1, :]
            m_prev = m_scr[h, 0:1, 0:1]
            q_h = q_c[:, h * M_DIM:(h + 1) * M_DIM]
            k_h = k_c[:, h * M_DIM:(h + 1) * M_DIM]
            vt_ext = jnp.concatenate([vt_all[h * M_DIM:(h + 1) * M_DIM, r0:r0 + L], ones_rows], axis=0)

            b_last = b_row[:, L - 1:L]
            g_row = b_last - b_row + i_row
            m_new = jnp.maximum(b_last + m_prev, jnp.max(g_row, axis=1, keepdims=True))
            w_state = jnp.exp(g_row - m_new)
            decay = jnp.exp(b_last + m_prev - m_new)

            s_t = _nt_dot(k_h, q_h)
            c_old = c_scr[h]
            inter = _nt_dot(c_old.astype(BF16), q_h)
            c_scr[h] = decay * c_old + _dot((vt_ext * w_state).astype(BF16), k_h)
            m_scr[h] = jnp.broadcast_to(m_new, (8, LANES))
            staged.append((s_t, inter, vt_ext, b_row, m_prev))

        heads = []
        for h in range(M_HEADS):
            s_t, inter, vt_ext, b_row, m_prev = staged[h]
            col = col_t[:, 4 + h:5 + h]
            dmat = jnp.where(keep, b_row + col, -jnp.inf)
            m_t = jnp.maximum(b_row + m_prev, jnp.max(dmat, axis=0, keepdims=True))
            sc = (s_t * jnp.exp(dmat - m_t)).astype(BF16)
            a_inter = jnp.exp(b_row + m_prev - m_t)
            num = a_inter * inter + _dot(vt_ext.astype(BF16), sc)
            den = num[M_DIM:M_DIM + 1, :]
            hh = num[:M_DIM, :] * (1.0 / jnp.maximum(jnp.abs(den), jnp.exp(-m_t)))
            inv = lax.rsqrt(jnp.mean(hh * hh, axis=0, keepdims=True) + EPS)
            heads.append(hh * inv * ngt_ref[h * M_DIM:(h + 1) * M_DIM, :])
        hcat = jnp.concatenate(heads, axis=0).T
        gate = _sigmoid(o_ref[0, r0:r0 + L, :].astype(F32))
        out_ref[0, r0:r0 + L, :] = (gate * hcat).astype(BF16)


def _mlstm(qk, mv, mo, gate_r, conv_w, conv_b, bias_r, norm_g, T=1024):
    B, S, _ = qk.shape
    row = lambda width: pl.BlockSpec((1, T, width), lambda b, t: (b, t, 0))
    const = lambda shape: pl.BlockSpec(shape, lambda b, t: (0,) * len(shape))
    return pl.pallas_call(
        functools.partial(_mlstm_kernel, T=T),
        grid=(B, S // T),
        in_specs=[row(512), row(256), row(256),
                  pl.BlockSpec((1, 8, T), lambda b, t: (b, 0, t)),
                  const((CONV_WIDTH, 2 * M_WIDTH)), const((1, 2 * M_WIDTH)),
                  const((8, 1)), const((M_WIDTH, LANES))],
        out_specs=row(M_WIDTH),
        out_shape=jax.ShapeDtypeStruct((B, S, M_WIDTH), BF16),
        scratch_shapes=[pltpu.VMEM((T + 8, 2 * M_WIDTH), F32),
                        pltpu.VMEM((M_HEADS, LANES, M_DIM), F32),
                        pltpu.VMEM((M_HEADS, 8, LANES), F32)],
        compiler_params=pltpu.CompilerParams(
            dimension_semantics=("arbitrary", "arbitrary"), vmem_limit_bytes=VMEM_LIMIT),
        name="mlstm",
    )(qk, mv, mo, gate_r, conv_w, conv_b, bias_r, norm_g)


def _swa_kernel(sink_ref, qt_ref, kc_ref, kp_ref, vtc_ref, vtp_ref, out_ref, s_scr, *, TQ):
    W = WINDOW
    G = S_HEADS // S_KV_HEADS
    tile = pl.program_id(1)
    c = lax.broadcasted_iota(jnp.int32, (2 * W, G * W), 0)
    r = lax.broadcasted_iota(jnp.int32, (2 * W, G * W), 1) % W
    band = (c > r) & (c <= r + W)
    units = [(j, hk) for j in range(TQ // W) for hk in range(S_KV_HEADS)]
    n_slots = s_scr.shape[0]

    def scores(u):
        j, hk = units[u]
        if j == 0:
            kband = jnp.concatenate([kp_ref[0], kc_ref[0, 0:W, :]], axis=0)
        else:
            kband = kc_ref[0, (j - 1) * W:(j + 1) * W, :]
        qs = jnp.concatenate(
            [qt_ref[0, (hk * G + g) * HEAD_DIM:(hk * G + g + 1) * HEAD_DIM, j * W:(j + 1) * W] for g in range(G)],
            axis=1)
        zeros = jnp.zeros_like(qs)
        qpad = jnp.concatenate([qs, zeros] if hk == 0 else [zeros, qs], axis=0)
        s_scr[u % n_slots] = _dot(kband, qpad)

    def softmax_pv(u):
        j, hk = units[u]
        rows = slice(hk * LANES, (hk + 1) * LANES)
        if j == 0:
            vt = jnp.concatenate([vtp_ref[0, rows, :], vtc_ref[0, rows, 0:W]], axis=1)
            valid = band & ((tile > 0) | (c >= W))
        else:
            vt = vtc_ref[0, rows, (j - 1) * W:(j + 1) * W]
            valid = band
        sink = jnp.concatenate([jnp.full((1, W), sink_ref[hk * G + g], F32) for g in range(G)], axis=1)
        s = jnp.where(valid, s_scr[u % n_slots], -jnp.inf)
        mx = jnp.maximum(jnp.max(s, axis=0, keepdims=True), sink)
        e = jnp.exp(s - mx).astype(BF16)
        o = _dot(vt, e)
        denom = o[HEAD_DIM:HEAD_DIM + 1, :] + jnp.exp(sink - mx)
        o = o[:HEAD_DIM, :] * (1.0 / denom)
        return [o[:, g * W:(g + 1) * W] for g in range(G)]

    scores(0)
    scores(1)
    heads = []
    for u in range(len(units)):
        if u + 2 < len(units):
            scores(u + 2)
        heads.extend(softmax_pv(u))
        if len(heads) == S_HEADS:
            j = units[u][0]
            out_ref[0, j * W:(j + 1) * W, :] = jnp.concatenate(heads, axis=0).T.astype(BF16)
            heads = []


def _swa(sqt, sk, svt, sinks, TQ=512):
    B, _, S = sqt.shape
    per = TQ // WINDOW
    prev_blk = lambda t: jnp.maximum(t * per - 1, 0)
    return pl.pallas_call(
        functools.partial(_swa_kernel, TQ=TQ),
        grid=(B, S // TQ),
        in_specs=[pl.BlockSpec(memory_space=pltpu.SMEM),
                  pl.BlockSpec((1, S_WIDTH, TQ), lambda b, t: (b, 0, t)),
                  pl.BlockSpec((1, TQ, S_KV_WIDTH), lambda b, t: (b, t, 0)),
                  pl.BlockSpec((1, WINDOW, S_KV_WIDTH), lambda b, t: (b, prev_blk(t), 0)),
                  pl.BlockSpec((1, S_KV_HEADS * LANES, TQ), lambda b, t: (b, 0, t)),
                  pl.BlockSpec((1, S_KV_HEADS * LANES, WINDOW), lambda b, t: (b, 0, prev_blk(t)))],
        out_specs=pl.BlockSpec((1, TQ, S_WIDTH), lambda b, t: (b, t, 0)),
        out_shape=jax.ShapeDtypeStruct((B, S, S_WIDTH), BF16),
        scratch_shapes=[pltpu.VMEM((3, 2 * WINDOW, (S_HEADS // S_KV_HEADS) * WINDOW), F32)],
        compiler_params=pltpu.CompilerParams(
            dimension_semantics=("arbitrary", "arbitrary"), vmem_limit_bytes=VMEM_LIMIT),
        name="swa",
    )(sinks, sqt, sk, sk, svt, svt)


def _diff_kernel(lamp_ref, subg_ref, q_ref, k_ref, vt_ref, out_ref, s_scr, m_scr, acc_scr, *, TQ, TK, lam_init):
    qi = pl.program_id(2)
    q = q_ref[0]
    lane = lax.broadcasted_iota(jnp.int32, (TQ, LANES), 1)
    zero = jnp.zeros_like(q)
    qg = [jnp.where((lane >= g * D_QK) & (lane < (g + 1) * D_QK), q, zero) for g in range(4)]

    m_scr[...] = jnp.full_like(m_scr, -jnp.inf)
    acc_scr[...] = jnp.zeros_like(acc_scr)

    assert TQ == TK
    below_diag = (lax.broadcasted_iota(jnp.int32, (TK, TQ), 0)
                  <= lax.broadcasted_iota(jnp.int32, (TK, TQ), 1))

    def scores(j, g):
        start = pl.multiple_of(j * TK, TK)
        s_scr[g] = _nt_dot(k_ref[0, pl.ds(start, TK), :], qg[g])

    def softmax_pv(j, g, masked):
        start = pl.multiple_of(j * TK, TK)
        vt = vt_ref[0, (g // 2) * D_VX:(g // 2 + 1) * D_VX, pl.ds(start, TK)]
        for hq in range(2):
            ql = slice(hq * (TQ // 2), (hq + 1) * (TQ // 2))
            s = s_scr[g, :, ql]
            if masked:
                s = jnp.where(below_diag[:, ql], s, -jnp.inf)
            m_old = m_scr[g, :, ql]
            m_new = jnp.maximum(m_old, jnp.max(s, axis=0, keepdims=True))
            p = jnp.exp2(s - m_new[0:1, :]).astype(BF16)
            alpha = jnp.exp2(m_old[0:1, :] - m_new[0:1, :])
            acc_scr[g, :, ql] = alpha * acc_scr[g, :, ql] + _dot(vt, p)
            m_scr[g, :, ql] = m_new

    def pipelined(j, masked, prefetch_next):
        scores(j, 2)
        softmax_pv(j, 0, masked)
        scores(j, 3)
        softmax_pv(j, 1, masked)
        if prefetch_next:
            scores(j + 1, 0)
        softmax_pv(j, 2, masked)
        if prefetch_next:
            scores(j + 1, 1)
        softmax_pv(j, 3, masked)

    n_full = (qi * TQ) // TK
    scores(0, 0)
    scores(0, 1)

    def unrolled_body(jj, carry):
        for u in range(KV_UNROLL):
            pipelined(KV_UNROLL * jj + u, False, True)
        return carry

    lax.fori_loop(0, n_full // KV_UNROLL, unrolled_body, 0)
    rem = n_full % KV_UNROLL
    for u in range(KV_UNROLL - 1):
        @pl.when(rem > u)
        def _():
            pipelined(n_full - rem + u, False, True)

    pipelined(n_full, True, False)

    lamp = lamp_ref[...]
    lam = (jnp.exp(jnp.sum(lamp[0:1] * lamp[1:2], axis=-1, keepdims=True))
           - jnp.exp(jnp.sum(lamp[2:3] * lamp[3:4], axis=-1, keepdims=True)) + lam_init)
    outs = []
    g_t = jnp.concatenate([subg_ref[...]] * (TQ // LANES), axis=1) * (1.0 - lam_init)
    for hd in range(2):
        a1 = acc_scr[2 * hd]
        a2 = acc_scr[2 * hd + 1]
        o = a1[:D_V, :] * (1.0 / a1[D_V:D_V + 1, :]) - lam * (a2[:D_V, :] * (1.0 / a2[D_V:D_V + 1, :]))
        outs.append(o * lax.rsqrt(jnp.mean(o * o, axis=0, keepdims=True) + EPS) * g_t)
    out_ref[0] = jnp.concatenate(outs, axis=0).T.astype(BF16)


def _diffattn(dq, dk, dvt, lamp, sub_g, lam_init, TQ=512, TK=512):
    B, S, _ = dq.shape
    return pl.pallas_call(
        functools.partial(_diff_kernel, TQ=TQ, TK=TK, lam_init=lam_init),
        grid=(B, D_HEADS // 2, S // TQ),
        in_specs=[pl.BlockSpec((4, LANES), lambda b, hp, i: (0, 0)),
                  pl.BlockSpec((D_V, LANES), lambda b, hp, i: (0, 0)),
                  pl.BlockSpec((1, TQ, LANES), lambda b, hp, i: (b, i, hp)),
                  pl.BlockSpec((1, S, LANES), lambda b, hp, i: (b, 0, hp)),
                  pl.BlockSpec((1, 2 * D_VX, S), lambda b, hp, i: (b, hp, 0))],
        out_specs=pl.BlockSpec((1, TQ, LANES), lambda b, hp, i: (b, i, hp)),
        out_shape=jax.ShapeDtypeStruct((B, S, D_WIDTH), BF16),
        scratch_shapes=[pltpu.VMEM((4, TK, TQ), F32), pltpu.VMEM((4, 8, TQ), F32),
                        pltpu.VMEM((4, D_VX, TQ), F32)],
        compiler_params=pltpu.CompilerParams(
            dimension_semantics=("arbitrary", "arbitrary", "arbitrary"), vmem_limit_bytes=VMEM_LIMIT),
        name="diffattn",
    )(lamp, sub_g, dq, dk, dvt)


def _out_mlp_kernel(x_ref, om_ref, os_ref, od_ref, wo_ref, wup_ref, wdn_ref,
                    gpm_ref, gpre_ref, gpost_ref, out_ref, *, FC, PARTS):
    half = x_ref.shape[0] // PARTS
    halves = [slice(p * half, (p + 1) * half) for p in range(PARTS)]

    def out_proj(rows):
        return (_dot(om_ref[rows, :], wo_ref[0:M_WIDTH, :])
                + _dot(os_ref[rows, :], wo_ref[M_WIDTH:M_WIDTH + S_WIDTH, :])
                + _dot(od_ref[rows, :], wo_ref[M_WIDTH + S_WIDTH:, :]))

    def mlp(h2):
        acc = jnp.zeros((half, D_MODEL), F32)
        for c in range(D_FF // FC):
            u = jnp.maximum(_dot(h2, wup_ref[:, c * FC:(c + 1) * FC]), 0.0)
            acc = acc + _dot((u * u).astype(BF16), wdn_ref[c * FC:(c + 1) * FC, :])
        return acc

    mixes = [out_proj(rows) for rows in halves]
    for rows, mix in zip(halves, mixes):
        x1 = x_ref[rows, :] + _rms(mix, gpm_ref[...])
        acc = mlp(_rms(x1, gpre_ref[...]).astype(BF16))
        out_ref[rows, :] = x1 + _rms(acc, gpost_ref[...])


def _out_mlp(x2d, om, os_, od, w_out, w_up, w_down, g_post_mix, g_pre_mlp, g_post_mlp, tm=1024, FC=1024, PARTS=4):
    N = x2d.shape[0]
    row = lambda width: pl.BlockSpec((tm, width), lambda i: (i, 0))
    const = lambda shape: pl.BlockSpec(shape, lambda i: (0, 0), pipeline_mode=pl.Buffered(1))
    return pl.pallas_call(
        functools.partial(_out_mlp_kernel, FC=FC, PARTS=PARTS),
        grid=(N // tm,),
        in_specs=[row(D_MODEL), row(M_WIDTH), row(S_WIDTH), row(D_WIDTH),
                  const((D_MODEL, D_MODEL)), const((D_MODEL, D_FF)), const((D_FF, D_MODEL)),
                  const((1, D_MODEL)), const((1, D_MODEL)), const((1, D_MODEL))],
        out_specs=row(D_MODEL),
        out_shape=jax.ShapeDtypeStruct((N, D_MODEL), F32),
        compiler_params=pltpu.CompilerParams(
            dimension_semantics=("arbitrary",), vmem_limit_bytes=VMEM_LIMIT),
        name="out_mlp",
    )(x2d, om, os_, od, w_out, w_up, w_down, g_post_mix, g_pre_mlp, g_post_mlp)


def _rope_tables(S):
    pos = jnp.arange(S, dtype=F32)[:, None]
    lane = np.arange(LANES)

    def table(head_dim):
        half = head_dim // 2
        inv = ROPE_THETA ** (-jnp.arange(half, dtype=F32) * 2.0 / head_dim)
        ang = pos * inv[None, :]
        idx = (lane % head_dim) % half
        sign = np.where((lane % head_dim) < half, -1.0, 1.0).astype(np.float32)
        return jnp.cos(ang)[:, idx], jnp.sin(ang)[:, idx] * sign

    c64, s64 = table(HEAD_DIM)
    c32, s32 = table(D_QK)
    inv = ROPE_THETA ** (-jnp.arange(HEAD_DIM // 2, dtype=F32) * 2.0 / HEAD_DIM)
    ang_t = inv[:, None] * pos.T
    return c64, s64, c32, s32, jnp.cos(ang_t), jnp.sin(ang_t)


def _pack_w_in(w_in):
    sizes = (M_WIDTH, M_WIDTH, M_WIDTH, M_WIDTH, M_HEADS, M_HEADS,
             S_WIDTH, S_KV_WIDTH, S_KV_WIDTH, D_QK_WIDTH, D_QK_WIDTH, D_WIDTH)
    offs = np.concatenate([[0], np.cumsum(sizes)])
    part = [w_in[:, offs[i]:offs[i + 1]] for i in range(len(sizes))]
    mq, mk, mv, mo, mi, mf, sq, sk, sv, dq, dk, dv = part
    zeros = lambda n: jnp.zeros((D_MODEL, n), w_in.dtype)
    dv_blocks = []
    for h in range(D_HEADS):
        dv_blocks += [dv[:, h * D_V:(h + 1) * D_V], zeros(D_VX - D_V)]
    sv_blocks = []
    for h in range(S_KV_HEADS):
        sv_blocks += [sv[:, h * HEAD_DIM:(h + 1) * HEAD_DIM], zeros(LANES - HEAD_DIM)]
    packed = jnp.concatenate([mq, mk, mv, mo, sk, dq, dk], axis=1)
    wg_t = jnp.concatenate([mi, mf], axis=1).T
    wdv_t = jnp.concatenate(dv_blocks, axis=1).T
    wsv_t = jnp.concatenate(sv_blocks, axis=1).T
    return tuple(w.astype(BF16) for w in (packed, wg_t, wdv_t, sq.T, wsv_t))


def kernel(x, w_in, conv_w, conv_b, i_bias, f_bias, m_norm_g, sinks, lam_q1, lam_k1, lam_q2, lam_k2,
           sub_g, w_out, w_up, w_down, g_pre_mix, g_post_mix, g_pre_mlp, g_post_mlp):
    B, S, D = x.shape
    depth = w_in.shape[0]
    tables = _rope_tables(S)
    pad_lanes = lambda v: jnp.pad(v, (0, LANES - v.shape[0]))[None, :]
    lane_bcast = lambda v: jnp.broadcast_to(v[:, None], (v.shape[0], LANES))
    for l in range(depth):
        (qk, mv, mo, gate_r, sqt, sk, svt, dq, dk, dvt) = _in_proj(
            x, g_pre_mix[l][None, :], *_pack_w_in(w_in[l]), tables)

        gate_bias = jnp.concatenate([i_bias[l], f_bias[l]])
        out_m = _mlstm(qk, mv, mo, gate_r, conv_w[l], conv_b[l][None, :],
                       gate_bias[:, None], lane_bcast(m_norm_g[l]))
        out_s = _swa(sqt, sk, svt, sinks[l])
        lamp = jnp.concatenate([pad_lanes(v) for v in (lam_q1[l], lam_k1[l], lam_q2[l], lam_k2[l])], axis=0)
        lam_init = 0.8 - 0.6 * math.exp(-0.3 * l)
        out_d = _diffattn(dq, dk, dvt, lamp, lane_bcast(sub_g[l]), lam_init)

        x = _out_mlp(x.reshape(B * S, D), out_m.reshape(B * S, -1), out_s.reshape(B * S, -1),
                     out_d.reshape(B * S, -1), w_out[l].astype(BF16), w_up[l].astype(BF16),
                     w_down[l].astype(BF16), g_post_mix[l][None, :], g_pre_mlp[l][None, :],
                     g_post_mlp[l][None, :]).reshape(B, S, D)
    return x
```
